```python
import math
import jax, jax.numpy as jnp
from jax import lax
import numpy as np

D_MODEL = 1024
BATCH = 8
SEQ = 2048
DEPTH = 4
DEC_BATCH = 32
DEC_SEQ = 1
PAST_LEN = 8192
PAGE_SIZE = 128

N_A_LAYERS = DEPTH // 2
N_B_LAYERS = DEPTH - N_A_LAYERS
CHUNK = 128
D_A = D_MODEL
A_GROUPS = 8
A_GROUP_DIM = D_A // A_GROUPS
N_HEADS = 8
HEAD_DIM = D_MODEL // (2 * N_HEADS)
V_DIM = 2 * HEAD_DIM
ROT_DIM = HEAD_DIM // 4
ROPE_THETA = 500000.0
Q_BLOCK = 128
N_EXPERTS = 16
N_EXPERT_GROUPS = 4
EXPERTS_PER_GROUP = N_EXPERTS // N_EXPERT_GROUPS
TOP_K = 2
D_EXPERT = 512
DEEPNORM_ALPHA = (2 * DEPTH) ** 0.25
DEEPNORM_BETA = (8 * DEPTH) ** -0.25
LN_EPS = 1e-5

kernel_name = 'yoco_gmlp_diffattn_moe_step'


def layer_norm(x, g, b):
    xf = x.astype(jnp.float32)
    mu = xf.mean(-1, keepdims=True)
    var = jnp.square(xf - mu).mean(-1, keepdims=True)
    return ((xf - mu) * lax.rsqrt(var + LN_EPS) * g.astype(jnp.float32) + b.astype(jnp.float32)).astype(x.dtype)


def rms_norm(x, g):
    xf = x.astype(jnp.float32)
    ms = jnp.square(xf).mean(-1, keepdims=True)
    return (xf * lax.rsqrt(ms + LN_EPS) * g.astype(jnp.float32)).astype(x.dtype)


def rope(x, pos):
    inv = 1.0 / (ROPE_THETA ** (jnp.arange(0, ROT_DIM, 2, dtype=jnp.float32) / ROT_DIM))
    ang = pos.astype(jnp.float32)[:, None] * inv[None, :]
    cos = jnp.cos(ang)[None, :, None, None, :]
    sin = jnp.sin(ang)[None, :, None, None, :]
    xf = x.astype(jnp.float32)
    half = ROT_DIM // 2
    x1 = xf[..., :half]
    x2 = xf[..., half:ROT_DIM]
    out = jnp.concatenate([x1 * cos - x2 * sin, x2 * cos + x1 * sin, xf[..., ROT_DIM:]], axis=-1)
    return out.astype(x.dtype)


def ada_mod(c, w, b):
    m = jnp.einsum('bd,de->be', jax.nn.silu(c), w) + b
    return jnp.split(m[:, None, :], 6, axis=-1)


def chunk_gmlp(h, w_in, ln_g, ln_b, w_s, b_s, w_out):
    Bsz, T, _ = h.shape
    z = jax.nn.gelu(jnp.einsum('btd,de->bte', h, w_in))
    u, v = jnp.split(z, 2, axis=-1)
    v = layer_norm(v, ln_g, ln_b)
    L = min(T, CHUNK)
    mask = jnp.tril(jnp.ones((L, L), dtype=bool))
    ws = jnp.where(mask[None], w_s[:, :L, :L], 0).astype(v.dtype)
    vg = v.reshape(Bsz, T // L, L, A_GROUPS, A_GROUP_DIM)
    s = jnp.einsum('gij,bnjgc->bnigc', ws, vg) + b_s[:, :L].T[None, None, :, :, None]
    out = u * s.reshape(Bsz, T, D_A)
    return jnp.einsum('bte,ed->btd', out, w_out), v


def diff_lambda(lq1, lk1, lq2, lk2, lam_init):
    f32 = lambda a: a.astype(jnp.float32)
    return jnp.exp(jnp.sum(f32(lq1) * f32(lk1))) - jnp.exp(jnp.sum(f32(lq2) * f32(lk2))) + lam_init


def diff_combine(scores, lam):
    p = jax.nn.softmax(scores, axis=-1)
    return p[:, :, 0] - lam * p[:, :, 1]


def diff_attn_prompt(q, k, v, lam):
    Bsz, S = q.shape[:2]
    nb = S // Q_BLOCK
    qb = q.reshape(Bsz, nb, Q_BLOCK, N_HEADS, 2, HEAD_DIM).swapaxes(0, 1)
    k_pos = jnp.arange(S)
    scale = HEAD_DIM ** -0.5

    def block(args):
        q_i, i = args
        q_pos = i * Q_BLOCK + jnp.arange(Q_BLOCK)
        s = jnp.einsum('bqhcd,bshcd->bhcqs', q_i, k, preferred_element_type=jnp.float32) * scale
        s = jnp.where(q_pos[:, None] >= k_pos[None, :], s, -jnp.inf)
        w = diff_combine(s, lam).astype(v.dtype)
        return jnp.einsum('bhqs,bshe->bqhe', w, v)

    o = lax.map(block, (qb, jnp.arange(nb)))
    return o.swapaxes(0, 1).reshape(Bsz, S, N_HEADS, V_DIM)


def diff_attn_sample(q, k_new, v_new, k_past, v_past, lam):
    T = q.shape[1]
    P = k_past.shape[1]
    scale = HEAD_DIM ** -0.5
    s_past = jnp.einsum('bqhcd,bshcd->bhcqs', q, k_past, preferred_element_type=jnp.float32) * scale
    s_new = jnp.einsum('bqhcd,bshcd->bhcqs', q, k_new, preferred_element_type=jnp.float32) * scale
    s_new = jnp.where(jnp.tril(jnp.ones((T, T), dtype=bool)), s_new, -jnp.inf)
    w = diff_combine(jnp.concatenate([s_past, s_new], axis=-1), lam).astype(v_new.dtype)
    return (jnp.einsum('bhqs,bshe->bqhe', w[..., :P], v_past)
            + jnp.einsum('bhqs,bshe->bqhe', w[..., P:], v_new))


def diff_out(o, subln_g, lam_init, w_o):
    o = rms_norm(o, subln_g) * (1.0 - lam_init)
    Bsz, T = o.shape[:2]
    return jnp.einsum('bte,ed->btd', o.reshape(Bsz, T, N_HEADS * V_DIM), w_o)


def moe(h, router_w, router_bias, w_gate, w_up, w_down):
    Bsz, T, D = h.shape
    t = h.reshape(Bsz * T, D)
    logits = jnp.einsum('td,de->te', t, router_w, preferred_element_type=jnp.float32)
    scores = jax.nn.softmax(logits, axis=-1)
    sel = scores + router_bias.astype(jnp.float32)
    group_score = lax.top_k(sel.reshape(-1, N_EXPERT_GROUPS, EXPERTS_PER_GROUP), TOP_K)[0].sum(-1)
    best_group = jnp.argmax(group_score, axis=-1)
    in_group = (jnp.arange(N_EXPERTS) // EXPERTS_PER_GROUP)[None, :] == best_group[:, None]
    _, idx = lax.top_k(jnp.where(in_group, sel, -jnp.inf), TOP_K)
    wk = jnp.take_along_axis(scores, idx, axis=-1)
    wk = wk / wk.sum(-1, keepdims=True)
    gate = jnp.sum(jax.nn.one_hot(idx, N_EXPERTS, dtype=jnp.float32) * wk[..., None], axis=1)
    a = jax.nn.silu(jnp.einsum('td,edf->tef', t, w_gate)) * jnp.einsum('td,edf->tef', t, w_up)
    a = a * gate.astype(a.dtype)[..., None]
    return jnp.einsum('tef,efd->td', a, w_down).reshape(Bsz, T, D)


def run_trunk(x, c, pos, attend, p):
    Bsz, T, _ = x.shape
    v_rows = []
    k_sh = None
    v_sh = None
    for l in range(DEPTH):
        sh1, sc1, g1, sh2, sc2, g2 = ada_mod(c, p['ada_w'][l], p['ada_b'][l])
        h = x * (1 + sc1) + sh1
        if l < N_A_LAYERS:
            out, v_a = chunk_gmlp(h, p['a_w_in'][l], p['a_ln_g'][l], p['a_ln_b'][l],
                                  p['a_w_s'][l], p['a_b_s'][l], p['a_w_out'][l])
            v_rows.append(v_a)
        else:
            j = l - N_A_LAYERS
            q = jnp.einsum('btd,de->bte', h, p['b_w_q'][j]).reshape(Bsz, T, N_HEADS, 2, HEAD_DIM)
            q = rope(q, pos)
            lam_init = 0.8 - 0.6 * math.exp(-0.3 * l)
            lam = diff_lambda(p['b_lam_q1'][j], p['b_lam_k1'][j], p['b_lam_q2'][j], p['b_lam_k2'][j], lam_init)
            o = attend(q, k_sh, v_sh, lam)
            out = diff_out(o, p['b_subln_g'][j], lam_init, p['b_w_o'][j])
        x = layer_norm(DEEPNORM_ALPHA * x + (1 + g1) * out, p['ln_g'][l, 0], p['ln_b'][l, 0])
        h = x * (1 + sc2) + sh2
        f = moe(h, p['router_w'], p['router_bias'], p['moe_w_gate'][l], p['moe_w_up'][l], p['moe_w_down'][l])
        x = layer_norm(DEEPNORM_ALPHA * x + (1 + g2) * f, p['ln_g'][l, 1], p['ln_b'][l, 1])
        if l == N_A_LAYERS - 1:
            k_sh = rope(jnp.einsum('btd,de->bte', x, p['kv_w_k']).reshape(Bsz, T, N_HEADS, 2, HEAD_DIM), pos)
            v_sh = jnp.einsum('btd,de->bte', x, p['kv_w_v']).reshape(Bsz, T, N_HEADS, V_DIM)
    return x, k_sh, v_sh, v_rows


def setup_inputs(seed: int = 0) -> dict:
    key = jax.random.key(seed)
    keys = iter(jax.random.split(key, 64))

    def nrm(shape, scale):
        return jax.random.normal(next(keys), shape, jnp.float32) * scale

    n_pages = PAST_LEN // PAGE_SIZE
    n_used = DEC_BATCH * n_pages
    n_pool = n_used + max(1, n_used // 4)
    perm = jax.random.permutation(next(keys), n_pool)
    page_table = perm[:n_used].reshape(DEC_BATCH, n_pages).astype(jnp.int32)
    qk_w = N_HEADS * 2 * HEAD_DIM
    v_w = N_HEADS * V_DIM
    D = D_MODEL
    return {
        'x_prompt': nrm((BATCH, SEQ, D), 1.0),
        'x_sample': nrm((DEC_BATCH, DEC_SEQ, D), 1.0),
        'cache_k': nrm((n_pool, PAGE_SIZE, N_HEADS, 2, HEAD_DIM), 1.0),
        'cache_v': nrm((n_pool, PAGE_SIZE, N_HEADS, V_DIM), DEEPNORM_BETA),
        'page_table': page_table,
        'c_prompt': nrm((BATCH, D), 1.0),
        'c_sample': nrm((DEC_BATCH, D), 1.0),
        'ada_w': nrm((DEPTH, D, 6 * D), 0.5 * D ** -0.5),
        'ada_b': nrm((DEPTH, 6 * D), 0.01),
        'ln_g': 1.0 + nrm((DEPTH, 2, D), 0.02),
        'ln_b': nrm((DEPTH, 2, D), 0.01),
        'a_w_in': nrm((N_A_LAYERS, D, 2 * D_A), D ** -0.5),
        'a_ln_g': 1.0 + nrm((N_A_LAYERS, D_A), 0.02),
        'a_ln_b': nrm((N_A_LAYERS, D_A), 0.01),
        'a_w_s': nrm((N_A_LAYERS, A_GROUPS, CHUNK, CHUNK), CHUNK ** -0.5),
        'a_b_s': 1.0 + nrm((N_A_LAYERS, A_GROUPS, CHUNK), 0.01),
        'a_w_out': nrm((N_A_LAYERS, D_A, D), D_A ** -0.5 * DEEPNORM_BETA),
        'kv_w_k': nrm((D, qk_w), D ** -0.5),
        'kv_w_v': nrm((D, v_w), D ** -0.5 * DEEPNORM_BETA),
        'b_w_q': nrm((N_B_LAYERS, D, qk_w), D ** -0.5),
        'b_lam_q1': nrm((N_B_LAYERS, HEAD_DIM), 0.1),
        'b_lam_k1': nrm((N_B_LAYERS, HEAD_DIM), 0.1),
        'b_lam_q2': nrm((N_B_LAYERS, HEAD_DIM), 0.1),
        'b_lam_k2': nrm((N_B_LAYERS, HEAD_DIM), 0.1),
        'b_subln_g': 1.0 + nrm((N_B_LAYERS, V_DIM), 0.02),
        'b_w_o': nrm((N_B_LAYERS, v_w, D), v_w ** -0.5 * DEEPNORM_BETA),
        'router_w': nrm((D, N_EXPERTS), D ** -0.5),
        'router_bias': nrm((N_EXPERTS,), 0.01),
        'moe_w_gate': nrm((DEPTH, N_EXPERTS, D, D_EXPERT), D ** -0.5),
        'moe_w_up': nrm((DEPTH, N_EXPERTS, D, D_EXPERT), D ** -0.5),
        'moe_w_down': nrm((DEPTH, N_EXPERTS, D_EXPERT, D), D_EXPERT ** -0.5 * DEEPNORM_BETA),
    }


def reference(x_prompt, x_sample, cache_k, cache_v, page_table, c_prompt, c_sample,
              ada_w, ada_b, ln_g, ln_b, a_w_in, a_ln_g, a_ln_b, a_w_s, a_b_s, a_w_out,
              kv_w_k, kv_w_v, b_w_q, b_lam_q1, b_lam_k1, b_lam_q2, b_lam_k2, b_subln_g, b_w_o,
              router_w, router_bias, moe_w_gate, moe_w_up, moe_w_down):
    p = dict(ada_w=ada_w, ada_b=ada_b, ln_g=ln_g, ln_b=ln_b, a_w_in=a_w_in, a_ln_g=a_ln_g,
             a_ln_b=a_ln_b, a_w_s=a_w_s, a_b_s=a_b_s, a_w_out=a_w_out, kv_w_k=kv_w_k, kv_w_v=kv_w_v,
             b_w_q=b_w_q, b_lam_q1=b_lam_q1, b_lam_k1=b_lam_k1, b_lam_q2=b_lam_q2, b_lam_k2=b_lam_k2,
             b_subln_g=b_subln_g, b_w_o=b_w_o, router_w=router_w, router_bias=router_bias,
             moe_w_gate=moe_w_gate, moe_w_up=moe_w_up, moe_w_down=moe_w_down)

    pos_prompt = jnp.arange(x_prompt.shape[1], dtype=jnp.int32)
    y_prompt, k_prompt, v_prompt, _ = run_trunk(x_prompt, c_prompt, pos_prompt, diff_attn_prompt, p)

    n_seq, n_pages = page_table.shape
    past = n_pages * PAGE_SIZE
    k_past = cache_k[page_table].reshape(n_seq, past, N_HEADS, 2, HEAD_DIM)
    v_past = cache_v[page_table].reshape(n_seq, past, N_HEADS, V_DIM)
    pos_sample = past + jnp.arange(x_sample.shape[1], dtype=jnp.int32)

    def attend_sample(q, k_new, v_new, lam):
        return diff_attn_sample(q, k_new, v_new, k_past, v_past, lam)

    y_sample, k_sample, v_sample, v_rows_sample = run_trunk(x_sample, c_sample, pos_sample, attend_sample, p)
    chunk_v_sample = jnp.stack(v_rows_sample, axis=0)
    return (y_prompt, y_sample, k_prompt, v_prompt, k_sample, v_sample, chunk_v_sample)
```

```python
import functools
import math

import jax
import jax.numpy as jnp
from jax import lax
from jax.experimental import pallas as pl
from jax.experimental.pallas import tpu as pltpu

DEPTH = 4
N_A_LAYERS = DEPTH // 2
CHUNK = 128
A_GROUPS = 8
N_HEADS = 8
HEAD_DIM = 64
V_DIM = 2 * HEAD_DIM
ROT_DIM = HEAD_DIM // 4
ROPE_THETA = 500000.0
N_EXPERTS = 16
N_EXPERT_GROUPS = 4
EXPERTS_PER_GROUP = N_EXPERTS // N_EXPERT_GROUPS
LN_EPS = 1e-5
DEEPNORM_ALPHA = (2 * DEPTH) ** 0.25
PAGE_SIZE = 128
N_MOD = 6

LANES = 128
BF16_SUBLANES = 16
VMEM_LIMIT_BYTES = 56 * 1024 * 1024

GATE_LANES = LANES
NEG_BIG = -1e30
PAGES_PER_STEP = 8

F32 = jnp.float32
BF16 = jnp.bfloat16


def _cparams(n_axes):
    return pltpu.CompilerParams(
        dimension_semantics=("arbitrary",) * n_axes, vmem_limit_bytes=VMEM_LIMIT_BYTES)


def _dot(a, b):
    return jnp.dot(a, b, preferred_element_type=F32)


def _dot_nt(a, b):
    return lax.dot_general(a, b, (((1,), (1,)), ((), ())), preferred_element_type=F32)


def _split_bf16(a):
    hi = a.astype(BF16)
    return hi, (a - hi.astype(F32)).astype(BF16)


def _dot3(x, w):
    x_hi, x_lo = _split_bf16(x)
    w_hi, w_lo = _split_bf16(w)
    m = x.shape[0]
    if m % BF16_SUBLANES == 0:
        r = _dot(jnp.concatenate([x_hi, x_lo], axis=0), w_hi)
        return r[:m] + r[m:] + _dot(x_hi, w_lo)
    return _dot(x_hi, w_hi) + _dot(x_lo, w_hi) + _dot(x_hi, w_lo)


def _mm(x, w_ref, precise):
    if precise:
        return _dot3(x, w_ref[...])
    return _dot(x.astype(BF16), w_ref[...])


def _layer_norm(x, g, b):
    mu = jnp.mean(x, axis=-1, keepdims=True)
    xc = x - mu
    var = jnp.mean(xc * xc, axis=-1, keepdims=True)
    return xc * lax.rsqrt(var + LN_EPS) * g + b


def _silu(x):
    return x * (1.0 / (1.0 + jnp.exp(-x)))


def _gelu_tanh(x):
    return x * (0.5 * (1.0 + jnp.tanh(0.7978845608028654 * (x + 0.044715 * (x * x * x)))))


def _tile_lanes(t, reps):
    return jnp.concatenate([t] * reps, axis=1)


def _rope(x, cos_t, sin_lo, sin_hi):
    reps = x.shape[1] // LANES
    n = x.shape[1]
    nxt = pltpu.roll(x, n - ROT_DIM // 2, axis=1)
    prv = pltpu.roll(x, ROT_DIM // 2, axis=1)
    return (x * _tile_lanes(cos_t, reps) + nxt * _tile_lanes(sin_lo, reps)
            + prv * _tile_lanes(sin_hi, reps))


def _ada_kernel(c_ref, w_ref, b_ref, o_ref):
    o_ref[...] = _dot3(_silu(c_ref[...]), w_ref[...]) + b_ref[...]


def _ada_call(c_all, ada_w, ada_b):
    depth, d, d6 = ada_w.shape
    r = c_all.shape[0]
    n_mod = d6 // d
    return pl.pallas_call(
        _ada_kernel,
        out_shape=jax.ShapeDtypeStruct((depth, n_mod, r, d), F32),
        grid=(depth, n_mod),
        in_specs=[
            pl.BlockSpec((r, d), lambda l, j: (0, 0)),
            pl.BlockSpec((None, d, d), lambda l, j: (l, 0, j)),
            pl.BlockSpec((None, None, 1, d), lambda l, j: (l, j, 0, 0)),
        ],
        out_specs=pl.BlockSpec((None, None, r, d), lambda l, j: (l, j, 0, 0)),
        compiler_params=_cparams(2),
        name="ada_mod",
    )(c_all, ada_w, ada_b.reshape(depth, n_mod, 1, d))


class _Group:
    def __init__(self, n_rows, tile, rows_per_cond, cond_row0, mods):
        self.n = n_rows
        self.tile = tile
        self.n_tiles = n_rows // tile
        self.per_row = rows_per_cond == 1
        self.precise = self.per_row
        self.mods = mods if self.per_row else mods.reshape(mods.shape[:3] + (1, mods.shape[3]))
        self.tiles_per_cond = None if self.per_row else rows_per_cond // tile
        self.cond_row0 = cond_row0
        self.d = mods.shape[-1]

    def mod_spec(self, layer, j):
        if self.per_row:
            return pl.BlockSpec((None, None, self.tile, self.d), lambda i, *_: (layer, j, 0, 0))
        tpc, r0 = self.tiles_per_cond, self.cond_row0
        return pl.BlockSpec((None, None, None, 1, self.d),
                            lambda i, *_: (layer, j, r0 + i // tpc, 0, 0))

    def row_spec(self, width):
        return pl.BlockSpec((self.tile, width), lambda i, *_: (i, 0))


def _full_spec(shape):
    nd = len(shape)
    return pl.BlockSpec(shape, lambda i, *_: (0,) * nd)


def _a_mixer_prompt_kernel(x_ref, sh_ref, sc_ref, gt_ref, win_ref, lng_ref, lnb_ref, ws_ref,
                           bst_ref, wout_ref, ng_ref, nb_ref, o_ref):
    x = x_ref[...]
    t = x.shape[0]
    d_a = lng_ref.shape[1]
    h = (x * (1.0 + sc_ref[...]) + sh_ref[...]).astype(BF16)
    z = _gelu_tanh(_dot(h, win_ref[...]))
    u = z[:, :d_a]
    v = _layer_norm(z[:, d_a:], lng_ref[...], lnb_ref[...]).astype(BF16)
    causal = (lax.broadcasted_iota(jnp.int32, (CHUNK, CHUNK), 0)
              >= lax.broadcasted_iota(jnp.int32, (CHUNK, CHUNK), 1))
    gd = d_a // A_GROUPS
    cols = []
    for g in range(A_GROUPS):
        wsg = jnp.where(causal, ws_ref[g], 0.0).astype(BF16)
        bias = bst_ref[:, g:g + 1]
        rows = [_dot(wsg, v[c * CHUNK:(c + 1) * CHUNK, g * gd:(g + 1) * gd]) + bias
                for c in range(t // CHUNK)]
        cols.append(jnp.concatenate(rows, axis=0))
    s = jnp.concatenate(cols, axis=1)
    y = _dot((u * s).astype(BF16), wout_ref[...])
    o_ref[...] = _layer_norm(DEEPNORM_ALPHA * x + (1.0 + gt_ref[...]) * y, ng_ref[...], nb_ref[...])


def _a_mixer_sample_kernel(x_ref, sh_ref, sc_ref, gt_ref, win_ref, lng_ref, lnb_ref, coef_ref,
                           bias_ref, wout_ref, ng_ref, nb_ref, o_ref, v_ref):
    x = x_ref[...]
    d_a = lng_ref.shape[1]
    h = x * (1.0 + sc_ref[...]) + sh_ref[...]
    z = _gelu_tanh(_dot3(h, win_ref[...]))
    u = z[:, :d_a]
    v = _layer_norm(z[:, d_a:], lng_ref[...], lnb_ref[...])
    v_ref[...] = v
    s = v * coef_ref[...] + bias_ref[...]
    y = _dot3(u * s, wout_ref[...])
    o_ref[...] = _layer_norm(DEEPNORM_ALPHA * x + (1.0 + gt_ref[...]) * y, ng_ref[...], nb_ref[...])


def _a_mixer_call(grp, x, layer, w_in, ln_g, ln_b, w_s, b_s, w_out, ng, nb):
    d = x.shape[1]
    d_a = ln_g.shape[0]
    common_in = [
        grp.row_spec(d), grp.mod_spec(layer, 0), grp.mod_spec(layer, 1), grp.mod_spec(layer, 2),
        _full_spec((d, 2 * d_a)), _full_spec((1, d_a)), _full_spec((1, d_a)),
    ]
    tail_in = [_full_spec((d_a, d)), _full_spec((1, d)), _full_spec((1, d))]
    args_head = (x, grp.mods, grp.mods, grp.mods, w_in, ln_g.reshape(1, d_a), ln_b.reshape(1, d_a))
    args_tail = (w_out, ng.reshape(1, d), nb.reshape(1, d))
    if grp.per_row:
        gd = d_a // A_GROUPS
        coef = jnp.repeat(w_s[:, 0, 0], gd).reshape(1, d_a)
        bias = jnp.repeat(b_s[:, 0], gd).reshape(1, d_a)
        return pl.pallas_call(
            _a_mixer_sample_kernel,
            out_shape=(jax.ShapeDtypeStruct((grp.n, d), F32), jax.ShapeDtypeStruct((grp.n, d_a), F32)),
            grid=(grp.n_tiles,),
            in_specs=common_in + [_full_spec((1, d_a)), _full_spec((1, d_a))] + tail_in,
            out_specs=(grp.row_spec(d), grp.row_spec(d_a)),
            compiler_params=_cparams(1),
            name="a_mixer_sample",
        )(*args_head, coef, bias, *args_tail)
    out = pl.pallas_call(
        _a_mixer_prompt_kernel,
        out_shape=jax.ShapeDtypeStruct((grp.n, d), F32),
        grid=(grp.n_tiles,),
        in_specs=common_in + [_full_spec((A_GROUPS, CHUNK, CHUNK)), _full_spec((CHUNK, A_GROUPS))] + tail_in,
        out_specs=grp.row_spec(d),
        compiler_params=_cparams(1),
        name="a_mixer_prompt",
    )(*args_head, w_s, b_s.T, *args_tail)
    return out, None


def _top2_of4(a, b, c, d):
    hi1, lo1 = jnp.maximum(a, b), jnp.minimum(a, b)
    hi2, lo2 = jnp.maximum(c, d), jnp.minimum(c, d)
    return jnp.maximum(hi1, hi2) + jnp.maximum(jnp.minimum(hi1, hi2), jnp.maximum(lo1, lo2))


def _first_argmax4(vals):
    m = jnp.maximum(jnp.maximum(vals[0], vals[1]), jnp.maximum(vals[2], vals[3]))
    idx = jnp.where(vals[0] == m, 0, jnp.where(vals[1] == m, 1, jnp.where(vals[2] == m, 2, 3)))
    return m, idx


def _router_kernel(x_ref, sh_ref, sc_ref, rwt_ref, rb_ref, hr_ref, gid_ref, rank_ref, cnt_ref,
                   carry_ref):
    i = pl.program_id(0)
    t, d = x_ref.shape

    @pl.when(i == 0)
    def _():
        carry_ref[...] = jnp.zeros_like(carry_ref)

    h = x_ref[...] * (1.0 + sc_ref[...]) + sh_ref[...]
    logits = lax.dot_general(rwt_ref[...], h, (((1,), (1,)), ((), ())),
                             precision=lax.Precision.HIGHEST, preferred_element_type=F32)
    mx = jnp.max(logits, axis=0, keepdims=True)
    ex = jnp.exp(logits - mx)
    scores = ex / jnp.sum(ex, axis=0, keepdims=True)
    sel = scores + rb_ref[...]
    sel_r = [sel[e:e + 1, :] for e in range(N_EXPERTS)]
    sc_r = [scores[e:e + 1, :] for e in range(N_EXPERTS)]
    epg = EXPERTS_PER_GROUP
    gs = [_top2_of4(*sel_r[g * epg:(g + 1) * epg]) for g in range(N_EXPERT_GROUPS)]
    best, gid = gs[0], jnp.zeros((1, t), jnp.int32)
    for g in range(1, N_EXPERT_GROUPS):
        upd = gs[g] > best
        gid = jnp.where(upd, g, gid)
        best = jnp.where(upd, gs[g], best)
    cand_sel, cand_sc = [], []
    for k in range(epg):
        a, b = sel_r[k], sc_r[k]
        for g in range(1, N_EXPERT_GROUPS):
            a = jnp.where(gid == g, sel_r[g * epg + k], a)
            b = jnp.where(gid == g, sc_r[g * epg + k], b)
        cand_sel.append(a)
        cand_sc.append(b)
    _, i1 = _first_argmax4(cand_sel)
    rest = [jnp.where(i1 == k, -jnp.inf, cand_sel[k]) for k in range(epg)]
    _, i2 = _first_argmax4(rest)
    w1 = sum(jnp.where(i1 == k, cand_sc[k], 0.0) for k in range(epg))
    w2 = sum(jnp.where(i2 == k, cand_sc[k], 0.0) for k in range(epg))
    tot = w1 + w2
    w1, w2 = w1 / tot, w2 / tot
    gate_rows = [jnp.where(i1 == k, w1, 0.0) + jnp.where(i2 == k, w2, 0.0) for k in range(epg)]
    gates_t = jnp.concatenate(gate_rows + [jnp.zeros((GATE_LANES - epg, t), F32)], axis=0)
    hr_ref[:, :d] = h
    hr_ref[:, d:] = gates_t.T
    onehot = jnp.concatenate(
        [(gid == g).astype(F32) for g in range(N_EXPERT_GROUPS)]
        + [jnp.zeros((8 - N_EXPERT_GROUPS, t), F32)], axis=0)
    before = (lax.broadcasted_iota(jnp.int32, (t, t), 0)
              < lax.broadcasted_iota(jnp.int32, (t, t), 1)).astype(BF16)
    rank_in_tile = _dot(onehot.astype(BF16), before)
    carry = carry_ref[...]
    rank = jnp.sum(onehot * (rank_in_tile + carry[:, :1]), axis=0, keepdims=True)
    gid_ref[...] = gid
    rank_ref[...] = rank.astype(jnp.int32)
    carry = carry + jnp.sum(onehot, axis=1, keepdims=True)
    carry_ref[...] = carry
    cnt_ref[...] = carry.astype(jnp.int32)


def _router_call(grp, x, layer, router_wt, router_b):
    d = x.shape[1]
    per_token = pl.BlockSpec((1, grp.tile), lambda i: (0, i))
    return pl.pallas_call(
        _router_kernel,
        out_shape=(jax.ShapeDtypeStruct((grp.n, d + GATE_LANES), F32),
                   jax.ShapeDtypeStruct((1, grp.n), jnp.int32),
                   jax.ShapeDtypeStruct((1, grp.n), jnp.int32),
                   jax.ShapeDtypeStruct((8, LANES), jnp.int32)),
        grid=(grp.n_tiles,),
        in_specs=[grp.row_spec(d), grp.mod_spec(layer, 3), grp.mod_spec(layer, 4),
                  _full_spec((N_EXPERTS, d)), _full_spec((N_EXPERTS, 1))],
        out_specs=(grp.row_spec(d + GATE_LANES), per_token, per_token, _full_spec((8, LANES))),
        scratch_shapes=[pltpu.VMEM((8, LANES), F32)],
        compiler_params=_cparams(1),
        name="moe_router",
    )(x, grp.mods, grp.mods, router_wt, router_b)


def _row_copy(src_ref, src_row, dst_ref, dst_row, sem):
    return pltpu.make_async_copy(src_ref.at[pl.ds(src_row, 1), :], dst_ref.at[pl.ds(dst_row, 1), :], sem)


def _scatter_kernel(pos_ref, gap_ref, hr_ref, out_ref, zero_ref, sem, zsem, *, tile):
    i = pl.program_id(0)
    base = i * tile

    def issue(r, c):
        _row_copy(hr_ref, r, out_ref, pos_ref[base + r], sem).start()
        return c

    lax.fori_loop(0, tile, issue, 0)

    @pl.when(i == pl.num_programs(0) - 1)
    def _():
        zero_ref[...] = jnp.zeros_like(zero_ref)
        for g in range(N_EXPERT_GROUPS):
            lo, hi = gap_ref[g], gap_ref[N_EXPERT_GROUPS + g]

            def zissue(r, c):
                _row_copy(zero_ref, 0, out_ref, r, zsem).start()
                return c

            def zwait(r, c):
                _row_copy(zero_ref, 0, out_ref, r, zsem).wait()
                return c

            lax.fori_loop(lo, hi, zissue, 0)
            lax.fori_loop(lo, hi, zwait, 0)

    pltpu.make_async_copy(hr_ref, out_ref.at[pl.ds(0, tile), :], sem).wait()


def _scatter_call(grp, hr, pos, gaps, n_rows_out):
    w = hr.shape[1]
    grid_spec = pltpu.PrefetchScalarGridSpec(
        num_scalar_prefetch=2,
        grid=(grp.n_tiles,),
        in_specs=[pl.BlockSpec((grp.tile, w), lambda i, *_: (i, 0))],
        out_specs=pl.BlockSpec(memory_space=pl.ANY),
        scratch_shapes=[pltpu.VMEM((8, w), F32), pltpu.SemaphoreType.DMA(()), pltpu.SemaphoreType.DMA(())],
    )
    return pl.pallas_call(
        functools.partial(_scatter_kernel, tile=grp.tile),
        out_shape=jax.ShapeDtypeStruct((n_rows_out, w), F32),
        grid_spec=grid_spec,
        compiler_params=_cparams(1),
        name="moe_scatter",
    )(pos, gaps, hr)


def _experts_kernel(grp_ref, na_ref, rows_ref, wg_ref, wu_ref, wd_ref, o_ref):
    j = pl.program_id(0)

    @pl.when(j >= na_ref[0])
    def _():
        o_ref[...] = jnp.zeros_like(o_ref)

    @pl.when(j < na_ref[0])
    def _():
        d = o_ref.shape[1]
        hb = rows_ref[:, :d].astype(BF16)
        acts = []
        for e in range(EXPERTS_PER_GROUP):
            g = _dot(hb, wg_ref[e])
            u = _dot(hb, wu_ref[e])
            acts.append((_silu(g) * u * rows_ref[:, d + e:d + e + 1]).astype(BF16))
        a = jnp.concatenate(acts, axis=1)
        f = wd_ref.shape[1]
        o_ref[...] = _dot(a, wd_ref[...].reshape(EXPERTS_PER_GROUP * f, d))


def _experts_call(rows, tile_grp, n_active, w_gate, w_up, w_down, expert_tile):
    n_rows, w = rows.shape
    e, d, f = w_gate.shape
    epg = EXPERTS_PER_GROUP
    grid_spec = pltpu.PrefetchScalarGridSpec(
        num_scalar_prefetch=2,
        grid=(n_rows // expert_tile,),
        in_specs=[
            pl.BlockSpec((expert_tile, w), lambda j, g, na: (jnp.minimum(j, na[0] - 1), 0)),
            pl.BlockSpec((None, epg, d, f), lambda j, g, na: (g[j], 0, 0, 0)),
            pl.BlockSpec((None, epg, d, f), lambda j, g, na: (g[j], 0, 0, 0)),
            pl.BlockSpec((None, epg, f, d), lambda j, g, na: (g[j], 0, 0, 0)),
        ],
        out_specs=pl.BlockSpec((expert_tile, d), lambda j, g, na: (j, 0)),
    )
    return pl.pallas_call(
        _experts_kernel,
        out_shape=jax.ShapeDtypeStruct((n_rows, d), F32),
        grid_spec=grid_spec,
        compiler_params=_cparams(1),
        name="moe_experts",
    )(tile_grp, n_active, rows,
      w_gate.reshape(e // epg, epg, d, f), w_up.reshape(e // epg, epg, d, f),
      w_down.reshape(e // epg, epg, f, d))


def _experts_precise_kernel(grp_ref, na_ref, rows_ref, wg_ref, wu_ref, wd_ref, o_ref):
    j = pl.program_id(0)
    e = pl.program_id(1)

    @pl.when(e == 0)
    def _():
        o_ref[...] = jnp.zeros_like(o_ref)

    @pl.when(j < na_ref[0])
    def _():
        d = o_ref.shape[1]
        h = rows_ref[:, :d]
        gate = sum(jnp.where(e == k, rows_ref[:, d + k:d + k + 1], 0.0) for k in range(EXPERTS_PER_GROUP))
        a = _silu(_dot3(h, wg_ref[...])) * _dot3(h, wu_ref[...]) * gate
        o_ref[...] += _dot3(a, wd_ref[...])


def _experts_precise_call(rows, tile_grp, n_active, w_gate, w_up, w_down, expert_tile):
    n_rows, w = rows.shape
    e, d, f = w_gate.shape
    epg = EXPERTS_PER_GROUP

    def w_index(j, k, g, na):
        return (g[j], jnp.where(j < na[0], k, epg - 1), 0, 0)

    grid_spec = pltpu.PrefetchScalarGridSpec(
        num_scalar_prefetch=2,
        grid=(n_rows // expert_tile, epg),
        in_specs=[
            pl.BlockSpec((expert_tile, w), lambda j, k, g, na: (jnp.minimum(j, na[0] - 1), 0)),
            pl.BlockSpec((None, None, d, f), w_index),
            pl.BlockSpec((None, None, d, f), w_index),
            pl.BlockSpec((None, None, f, d), w_index),
        ],
        out_specs=pl.BlockSpec((expert_tile, d), lambda j, k, g, na: (j, 0)),
    )
    return pl.pallas_call(
        _experts_precise_kernel,
        out_shape=jax.ShapeDtypeStruct((n_rows, d), F32),
        grid_spec=grid_spec,
        compiler_params=_cparams(2),
        name="moe_experts_precise",
    )(tile_grp, n_active, rows,
      w_gate.reshape(e // epg, epg, d, f), w_up.reshape(e // epg, epg, d, f),
      w_down.reshape(e // epg, epg, f, d))


def _row_plan(counts, gid, rank, expert_tile, n_tiles):
    nt = (counts + expert_tile - 1) // expert_tile
    cum = jnp.cumsum(nt)
    starts = (cum - nt) * expert_tile
    pos = starts[gid] + rank
    gap_lo = starts + counts
    gap_hi = jnp.concatenate([starts[1:], jnp.full((1,), n_tiles * expert_tile, jnp.int32)])
    n_active = cum[-1]
    j = jnp.minimum(jnp.arange(n_tiles, dtype=jnp.int32), n_active - 1)
    tile_grp = jnp.sum((j[:, None] >= cum[None, :]).astype(jnp.int32), axis=1)
    gaps = jnp.concatenate([gap_lo, gap_hi]).astype(jnp.int32)
    return pos.astype(jnp.int32), gaps, tile_grp.astype(jnp.int32), n_active.reshape(1).astype(jnp.int32)


def _moe_post_kernel(pos_ref, x_ref, gt_ref, ng_ref, nb_ref, f_ref, o_ref, buf_ref, sem, *, tile):
    i = pl.program_id(0)
    n = pl.num_programs(0)

    def issue(step, slot):
        base = step * tile

        def body(r, c):
            _row_copy(f_ref, pos_ref[base + r], buf_ref.at[slot], r, sem.at[slot]).start()
            return c

        lax.fori_loop(0, tile, body, 0)

    @pl.when(i == 0)
    def _():
        issue(0, 0)

    slot = i % 2

    @pl.when(i + 1 < n)
    def _():
        issue(i + 1, 1 - slot)

    pltpu.make_async_copy(f_ref.at[pl.ds(0, tile), :], buf_ref.at[slot], sem.at[slot]).wait()
    x = x_ref[...]
    o_ref[...] = _layer_norm(DEEPNORM_ALPHA * x + (1.0 + gt_ref[...]) * buf_ref[slot],
                             ng_ref[...], nb_ref[...])


def _moe_post_call(grp, x, layer, f_rows, pos, ng, nb):
    d = x.shape[1]
    grid_spec = pltpu.PrefetchScalarGridSpec(
        num_scalar_prefetch=1,
        grid=(grp.n_tiles,),
        in_specs=[grp.row_spec(d), grp.mod_spec(layer, 5), _full_spec((1, d)), _full_spec((1, d)),
                  pl.BlockSpec(memory_space=pl.ANY)],
        out_specs=grp.row_spec(d),
        scratch_shapes=[pltpu.VMEM((2, grp.tile, d), F32), pltpu.SemaphoreType.DMA((2,))],
    )
    return pl.pallas_call(
        functools.partial(_moe_post_kernel, tile=grp.tile),
        out_shape=jax.ShapeDtypeStruct((grp.n, d), F32),
        grid_spec=grid_spec,
        compiler_params=_cparams(1),
        name="moe_post",
    )(pos, x, grp.mods, ng.reshape(1, d), nb.reshape(1, d), f_rows)


def _moe_sublayer(grp, x, layer, router_wt, router_b, w_gate, w_up, w_down, ng, nb, expert_tile):
    hr, gid, rank, cnt = _router_call(grp, x, layer, router_wt, router_b)
    n_tiles = grp.n // expert_tile + N_EXPERT_GROUPS
    pos, gaps, tile_grp, n_active = _row_plan(cnt[:N_EXPERT_GROUPS, 0], gid.reshape(grp.n),
                                              rank.reshape(grp.n), expert_tile, n_tiles)
    rows = _scatter_call(grp, hr, pos, gaps, n_tiles * expert_tile)
    experts = _experts_precise_call if grp.precise else _experts_call
    f_rows = experts(rows, tile_grp, n_active, w_gate, w_up, w_down, expert_tile)
    return _moe_post_call(grp, x, layer, f_rows, pos, ng, nb)


def _kv_kernel(x_ref, wk_ref, wv_ref, cos_ref, slo_ref, shi_ref, *out_refs, precise):
    x = x_ref[...]
    k = _rope(_mm(x, wk_ref, precise), cos_ref[...], slo_ref[...], shi_ref[...])
    v = _mm(x, wv_ref, precise)
    out_refs[0][...] = k
    out_refs[1][...] = v
    if len(out_refs) == 4:
        out_refs[2][...] = k.astype(BF16)
        out_refs[3][...] = v.astype(BF16)


def _rope_spec(grp, tables_rows):
    if tables_rows == 1:
        return _full_spec((1, LANES))
    tiles = tables_rows // grp.tile
    return pl.BlockSpec((grp.tile, LANES), lambda i, *_: (i % tiles, 0))


def _kv_call(grp, x, w_k, w_v, rope_tabs, with_bf16):
    d = x.shape[1]
    wk_cols, wv_cols = w_k.shape[1], w_v.shape[1]
    rs = _rope_spec(grp, rope_tabs[0].shape[0])
    out_shape = [jax.ShapeDtypeStruct((grp.n, wk_cols), F32), jax.ShapeDtypeStruct((grp.n, wv_cols), F32)]
    out_specs = [grp.row_spec(wk_cols), grp.row_spec(wv_cols)]
    if with_bf16:
        out_shape += [jax.ShapeDtypeStruct((grp.n, wk_cols), BF16), jax.ShapeDtypeStruct((grp.n, wv_cols), BF16)]
        out_specs += [grp.row_spec(wk_cols), grp.row_spec(wv_cols)]
    return pl.pallas_call(
        functools.partial(_kv_kernel, precise=grp.precise),
        out_shape=tuple(out_shape),
        grid=(grp.n_tiles,),
        in_specs=[grp.row_spec(d), _full_spec((d, wk_cols)), _full_spec((d, wv_cols)), rs, rs, rs],
        out_specs=tuple(out_specs),
        compiler_params=_cparams(1),
        name="kv_proj",
    )(x, w_k, w_v, *rope_tabs)


def _q_kernel(x_ref, sh_ref, sc_ref, wq_ref, cos_ref, slo_ref, shi_ref, o_ref, *, precise):
    h = x_ref[...] * (1.0 + sc_ref[...]) + sh_ref[...]
    q = _rope(_mm(h, wq_ref, precise), cos_ref[...], slo_ref[...], shi_ref[...])
    o_ref[...] = (q * (HEAD_DIM ** -0.5)).astype(o_ref.dtype)


def _q_call(grp, x, layer, w_q, rope_tabs, out_dtype):
    d = x.shape[1]
    cols = w_q.shape[1]
    rs = _rope_spec(grp, rope_tabs[0].shape[0])
    return pl.pallas_call(
        functools.partial(_q_kernel, precise=grp.precise),
        out_shape=jax.ShapeDtypeStruct((grp.n, cols), out_dtype),
        grid=(grp.n_tiles,),
        in_specs=[grp.row_spec(d), grp.mod_spec(layer, 0), grp.mod_spec(layer, 1),
                  _full_spec((d, cols)), rs, rs, rs],
        out_specs=grp.row_spec(cols),
        compiler_params=_cparams(1),
        name="q_proj",
    )(x, grp.mods, grp.mods, w_q, *rope_tabs)


def _diff_lambda(lamv_ref, lam_init):
    lv = lamv_ref[...]
    a = jnp.exp(jnp.sum(lv[0:1] * lv[1:2], axis=1, keepdims=True))
    b = jnp.exp(jnp.sum(lv[2:3] * lv[3:4], axis=1, keepdims=True))
    return a - b + lam_init


def _attn_prompt_kernel(q_ref, k_ref, v_ref, lamv_ref, sg_ref, o_ref, *, lam_init, tq):
    i = pl.program_id(2)
    q = q_ref[...]
    lane = lax.broadcasted_iota(jnp.int32, q.shape, 1)
    zero = jnp.zeros_like(q)
    qq = jnp.concatenate([jnp.where(lane < HEAD_DIM, q, zero), jnp.where(lane >= HEAD_DIM, q, zero)], axis=0)

    def step(j, carry, masked):
        m, l, acc = carry
        start = pl.multiple_of(j * tq, tq)
        kb = k_ref[pl.ds(start, tq), :]
        vb = v_ref[pl.ds(start, tq), :]
        s = _dot_nt(qq, kb)
        if masked:
            r = lax.broadcasted_iota(jnp.int32, s.shape, 0)
            r = jnp.where(r >= tq, r - tq, r)
            c = lax.broadcasted_iota(jnp.int32, s.shape, 1)
            s = jnp.where(c <= r, s, NEG_BIG)
        m_new = jnp.maximum(m, jnp.max(s, axis=1, keepdims=True))
        a = jnp.exp(m - m_new)
        p = jnp.exp(s - m_new)
        l = a * l + jnp.sum(p, axis=1, keepdims=True)
        acc = a * acc + _dot(p.astype(BF16), vb)
        return m_new, l, acc

    init = (jnp.full((2 * tq, 1), NEG_BIG, F32), jnp.zeros((2 * tq, 1), F32),
            jnp.zeros((2 * tq, V_DIM), F32))
    carry = lax.fori_loop(0, i, lambda j, c: step(j, c, False), init)
    _, l, acc = step(i, carry, True)
    o = acc / l
    od = o[:tq] - _diff_lambda(lamv_ref, lam_init) * o[tq:]
    ms = jnp.mean(od * od, axis=-1, keepdims=True)
    o_ref[...] = (od * lax.rsqrt(ms + LN_EPS) * sg_ref[...] * (1.0 - lam_init)).astype(o_ref.dtype)


def _attn_prompt_call(q, k, v, lamv, subln_g, lam_init, batch, seq, tq):
    n, cols = q.shape
    nq = seq // tq
    return pl.pallas_call(
        functools.partial(_attn_prompt_kernel, lam_init=lam_init, tq=tq),
        out_shape=jax.ShapeDtypeStruct((n, cols), BF16),
        grid=(batch, N_HEADS, nq),
        in_specs=[
            pl.BlockSpec((tq, V_DIM), lambda b, h, i: (b * nq + i, h)),
            pl.BlockSpec((seq, V_DIM), lambda b, h, i: (b, h)),
            pl.BlockSpec((seq, V_DIM), lambda b, h, i: (b, h)),
            pl.BlockSpec((4, HEAD_DIM), lambda b, h, i: (0, 0)),
            pl.BlockSpec((1, V_DIM), lambda b, h, i: (0, 0)),
        ],
        out_specs=pl.BlockSpec((tq, V_DIM), lambda b, h, i: (b * nq + i, h)),
        compiler_params=_cparams(3),
        name="attn_prompt",
    )(q, k, v, lamv, subln_g.reshape(1, V_DIM))


def _attn_sample_kernel(pt_ref, q_ref, kn_ref, vn_ref, lamv_ref, sg_ref, *rest, lam_init):
    del pt_ref
    npg = PAGES_PER_STEP
    k_refs, v_refs = rest[:npg], rest[npg:2 * npg]
    o_ref, m_ref, l_ref, acc_ref = rest[2 * npg:]
    step = pl.program_id(1)
    n_rows = 2 * N_HEADS
    cols = q_ref.shape[1]
    row = lax.broadcasted_iota(jnp.int32, (n_rows, cols), 0)
    lane = lax.broadcasted_iota(jnp.int32, (n_rows, cols), 1)
    seg_of_row = jnp.where(row >= N_HEADS, (row - N_HEADS) * 2 + 1, row * 2)
    own = (lane // HEAD_DIM) == seg_of_row
    qm32 = jnp.where(own, jnp.broadcast_to(q_ref[...], (n_rows, cols)), 0.0)
    qm = qm32.astype(BF16)

    @pl.when(step == 0)
    def _():
        m_ref[...] = jnp.sum(qm32 * kn_ref[...], axis=1, keepdims=True)
        l_ref[...] = jnp.ones_like(l_ref)
        acc_ref[...] = jnp.broadcast_to(vn_ref[...], acc_ref.shape)

    kcat = jnp.concatenate([r[...].astype(BF16) for r in k_refs], axis=0)
    vcat = jnp.concatenate([r[...].astype(BF16) for r in v_refs], axis=0)
    s = _dot_nt(qm, kcat)
    m_old = m_ref[...]
    m_new = jnp.maximum(m_old, jnp.max(s, axis=1, keepdims=True))
    a = jnp.exp(m_old - m_new)
    p = jnp.exp(s - m_new)
    l_ref[...] = a * l_ref[...] + jnp.sum(p, axis=1, keepdims=True)
    acc_ref[...] = a * acc_ref[...] + _dot(p.astype(BF16), vcat)
    m_ref[...] = m_new

    @pl.when(step == pl.num_programs(1) - 1)
    def _():
        o = acc_ref[...] / l_ref[...]
        od = o[:N_HEADS] - _diff_lambda(lamv_ref, lam_init) * o[N_HEADS:]
        hrow = lax.broadcasted_iota(jnp.int32, od.shape, 0)
        hlane = lax.broadcasted_iota(jnp.int32, od.shape, 1)
        mine = (hlane // V_DIM) == hrow
        od = jnp.where(mine, od, 0.0)
        ms = jnp.sum(od * od, axis=1, keepdims=True) * (1.0 / V_DIM)
        on = od * lax.rsqrt(ms + LN_EPS)
        o_ref[...] = jnp.sum(on, axis=0, keepdims=True) * sg_ref[...] * (1.0 - lam_init)


def _attn_sample_call(q, k_new, v_new, cache_k, cache_v, page_table, lamv, subln_g, lam_init):
    n_seq, cols = q.shape
    n_pool = cache_k.shape[0]
    n_pages = page_table.shape[1]
    npg = PAGES_PER_STEP
    ck = cache_k.reshape(n_pool, PAGE_SIZE, cols)
    cv = cache_v.reshape(n_pool, PAGE_SIZE, cols)
    row3 = lambda a: a.reshape(n_seq, 1, cols)
    row_spec = pl.BlockSpec((None, 1, cols), lambda b, g, pt: (b, 0, 0))

    def page_spec(i):
        return pl.BlockSpec((None, PAGE_SIZE, cols), lambda b, g, pt: (pt[b * n_pages + g * npg + i], 0, 0))

    grid_spec = pltpu.PrefetchScalarGridSpec(
        num_scalar_prefetch=1,
        grid=(n_seq, n_pages // npg),
        in_specs=[row_spec, row_spec, row_spec,
                  pl.BlockSpec((4, HEAD_DIM), lambda b, g, pt: (0, 0)),
                  pl.BlockSpec((1, cols), lambda b, g, pt: (0, 0))]
        + [page_spec(i) for i in range(npg)] * 2,
        out_specs=row_spec,
        scratch_shapes=[pltpu.VMEM((2 * N_HEADS, 1), F32), pltpu.VMEM((2 * N_HEADS, 1), F32),
                        pltpu.VMEM((2 * N_HEADS, cols), F32)],
    )
    out = pl.pallas_call(
        functools.partial(_attn_sample_kernel, lam_init=lam_init),
        out_shape=jax.ShapeDtypeStruct((n_seq, 1, cols), F32),
        grid_spec=grid_spec,
        compiler_params=_cparams(2),
        name="attn_sample",
    )(page_table.reshape(-1), row3(q), row3(k_new), row3(v_new), lamv,
      jnp.tile(subln_g, N_HEADS).reshape(1, cols), *([ck] * npg), *([cv] * npg))
    return out.reshape(n_seq, cols)


def _attn_out_kernel(o_ref, x_ref, gt_ref, wo_ref, ng_ref, nb_ref, y_ref, *, precise):
    y = _mm(o_ref[...], wo_ref, precise)
    x = x_ref[...]
    y_ref[...] = _layer_norm(DEEPNORM_ALPHA * x + (1.0 + gt_ref[...]) * y, ng_ref[...], nb_ref[...])


def _attn_out_call(grp, o, x, layer, w_o, ng, nb):
    d = x.shape[1]
    cols = o.shape[1]
    return pl.pallas_call(
        functools.partial(_attn_out_kernel, precise=grp.precise),
        out_shape=jax.ShapeDtypeStruct((grp.n, d), F32),
        grid=(grp.n_tiles,),
        in_specs=[grp.row_spec(cols), grp.row_spec(d), grp.mod_spec(layer, 2), _full_spec((cols, d)),
                  _full_spec((1, d)), _full_spec((1, d))],
        out_specs=grp.row_spec(d),
        compiler_params=_cparams(1),
        name="attn_out",
    )(o, x, grp.mods, w_o, ng.reshape(1, d), nb.reshape(1, d))


def _rope_tables(pos):
    half = ROT_DIM // 2
    inv = 1.0 / (ROPE_THETA ** (jnp.arange(0, ROT_DIM, 2, dtype=F32) / ROT_DIM))
    ang = pos.astype(F32)[:, None] * inv[None, :]
    cos, sin = jnp.cos(ang), jnp.sin(ang)
    n = pos.shape[0]
    ones = jnp.ones((n, HEAD_DIM - ROT_DIM), F32)
    zeros = jnp.zeros((n, HEAD_DIM - ROT_DIM), F32)
    zh = jnp.zeros((n, half), F32)
    cos_t = jnp.concatenate([cos, cos, ones], axis=1)
    sin_lo = jnp.concatenate([-sin, zh, zeros], axis=1)
    sin_hi = jnp.concatenate([zh, sin, zeros], axis=1)
    rep = LANES // HEAD_DIM
    return tuple(jnp.tile(t, (1, rep)) for t in (cos_t, sin_lo, sin_hi))


def _trunk(grp, x, p, rope_tabs, attend, expert_tile, want_bf16_kv):
    v_rows = []
    k_sh = v_sh = kv_attn = None
    for l in range(DEPTH):
        if l < N_A_LAYERS:
            x, v_a = _a_mixer_call(grp, x, l, p["a_w_in"][l], p["a_ln_g"][l], p["a_ln_b"][l],
                                   p["a_w_s"][l], p["a_b_s"][l], p["a_w_out"][l],
                                   p["ln_g"][l, 0], p["ln_b"][l, 0])
            v_rows.append(v_a)
        else:
            j = l - N_A_LAYERS
            lam_init = 0.8 - 0.6 * math.exp(-0.3 * l)
            q = _q_call(grp, x, l, p["b_w_q"][j], rope_tabs, BF16 if want_bf16_kv else F32)
            lamv = jnp.stack([p["b_lam_q1"][j], p["b_lam_k1"][j], p["b_lam_q2"][j], p["b_lam_k2"][j]])
            o = attend(q, kv_attn, lamv, p["b_subln_g"][j], lam_init)
            x = _attn_out_call(grp, o, x, l, p["b_w_o"][j], p["ln_g"][l, 0], p["ln_b"][l, 0])
        x = _moe_sublayer(grp, x, l, p["router_wt"], p["router_b"], p["moe_w_gate"][l], p["moe_w_up"][l],
                          p["moe_w_down"][l], p["ln_g"][l, 1], p["ln_b"][l, 1], expert_tile)
        if l == N_A_LAYERS - 1:
            outs = _kv_call(grp, x, p["kv_w_k"], p["kv_w_v"], rope_tabs, want_bf16_kv)
            k_sh, v_sh = outs[0], outs[1]
            kv_attn = (outs[2], outs[3]) if want_bf16_kv else (k_sh, v_sh)
    return x, k_sh, v_sh, v_rows


PROMPT_TILE = 512
PROMPT_EXPERT_TILE = 512
ATTN_Q_TILE = 256


def kernel(x_prompt, x_sample, cache_k, cache_v, page_table, c_prompt, c_sample, ada_w, ada_b, ln_g, ln_b, a_w_in, a_ln_g, a_ln_b, a_w_s, a_b_s, a_w_out, kv_w_k, kv_w_v, b_w_q, b_lam_q1, b_lam_k1, b_lam_q2, b_lam_k2, b_subln_g, b_w_o, router_w, router_bias, moe_w_gate, moe_w_up, moe_w_down):
    batch, seq, d = x_prompt.shape
    dec_batch, dec_seq, _ = x_sample.shape
    assert dec_seq == 1, "the sample group is one new token per sequence"
    past = page_table.shape[1] * PAGE_SIZE

    p_f32 = dict(
        a_w_in=a_w_in, a_ln_g=a_ln_g, a_ln_b=a_ln_b, a_w_s=a_w_s, a_b_s=a_b_s, a_w_out=a_w_out,
        ln_g=ln_g, ln_b=ln_b, kv_w_k=kv_w_k, kv_w_v=kv_w_v, b_w_q=b_w_q,
        b_lam_q1=b_lam_q1, b_lam_k1=b_lam_k1, b_lam_q2=b_lam_q2, b_lam_k2=b_lam_k2,
        b_subln_g=b_subln_g, b_w_o=b_w_o,
        router_wt=router_w.T, router_b=router_bias.reshape(N_EXPERTS, 1),
        moe_w_gate=moe_w_gate, moe_w_up=moe_w_up, moe_w_down=moe_w_down,
    )
    matmul_weights = ("a_w_in", "a_w_out", "kv_w_k", "kv_w_v", "b_w_q", "b_w_o",
                      "moe_w_gate", "moe_w_up", "moe_w_down")
    p_bf16 = {k: (v.astype(BF16) if k in matmul_weights else v) for k, v in p_f32.items()}

    mods = _ada_call(jnp.concatenate([c_sample, c_prompt], axis=0), ada_w, ada_b)

    gp = _Group(batch * seq, PROMPT_TILE, seq, dec_batch, mods)
    tabs_p = _rope_tables(jnp.arange(seq, dtype=jnp.int32))

    def attend_prompt(q, kv, lamv, subln_g, lam_init):
        return _attn_prompt_call(q, kv[0], kv[1], lamv, subln_g, lam_init, batch, seq, ATTN_Q_TILE)

    y_p, k_p, v_p, _ = _trunk(gp, x_prompt.reshape(batch * seq, d), p_bf16, tabs_p, attend_prompt,
                              PROMPT_EXPERT_TILE, True)

    gs = _Group(dec_batch, dec_batch, 1, 0, mods)
    tabs_s = _rope_tables(jnp.full((1,), past, dtype=jnp.int32))

    def attend_sample(q, kv, lamv, subln_g, lam_init):
        return _attn_sample_call(q, kv[0], kv[1], cache_k, cache_v, page_table, lamv, subln_g, lam_init)

    y_s, k_s, v_s, v_rows = _trunk(gs, x_sample.reshape(dec_batch, d), p_f32, tabs_s, attend_sample,
                                   dec_batch, False)

    d_a = a_ln_g.shape[1]
    return (
        y_p.reshape(batch, seq, d),
        y_s.reshape(dec_batch, 1, d),
        k_p.reshape(batch, seq, N_HEADS, 2, HEAD_DIM),
        v_p.reshape(batch, seq, N_HEADS, V_DIM),
        k_s.reshape(dec_batch, 1, N_HEADS, 2, HEAD_DIM),
        v_s.reshape(dec_batch, 1, N_HEADS, V_DIM),
        jnp.stack(v_rows, axis=0).reshape(N_A_LAYERS, dec_batch, 1, d_a),
    )
```

```python
import functools
import math

import jax
import jax.numpy as jnp
from jax import lax
from jax.experimental import pallas as pl
from jax.experimental.pallas import tpu as pltpu

DEPTH = 4
N_A_LAYERS = DEPTH // 2
CHUNK = 128
A_GROUPS = 8
N_HEADS = 8
HEAD_DIM = 64
V_DIM = 2 * HEAD_DIM
ROT_DIM = HEAD_DIM // 4
ROPE_THETA = 500000.0
N_EXPERTS = 16
N_EXPERT_GROUPS = 4
EXPERTS_PER_GROUP = N_EXPERTS // N_EXPERT_GROUPS
LN_EPS = 1e-5
DEEPNORM_ALPHA = (2 * DEPTH) ** 0.25
PAGE_SIZE = 128

LANES = 128
SUBLANES = 8
BF16_SUBLANES = 16
VMEM_LIMIT_BYTES = 56 * 1024 * 1024

GATE_LANES = LANES
NEG_BIG = -1e30
PAGES_PER_STEP = 8
DMA_ISSUE_UNROLL = 8
SOFTMAX_ROWS = 32

PROMPT_TILE = 512
PROMPT_EXPERT_TILE = 512
ATTN_TILE = 256

F32 = jnp.float32
BF16 = jnp.bfloat16


def _cparams(n_axes):
    return pltpu.CompilerParams(
        dimension_semantics=("arbitrary",) * n_axes, vmem_limit_bytes=VMEM_LIMIT_BYTES)


def _dot(a, b):
    return jnp.dot(a, b, preferred_element_type=F32)


def _dot_nt(a, b):
    return lax.dot_general(a, b, (((1,), (1,)), ((), ())), preferred_element_type=F32)


def _split_bf16(a):
    hi = a.astype(BF16)
    return hi, (a - hi.astype(F32)).astype(BF16)


def _dot3(x, w):
    x_hi, x_lo = _split_bf16(x)
    w_hi, w_lo = _split_bf16(w)
    m = x.shape[0]
    if m % BF16_SUBLANES == 0:
        r = _dot(jnp.concatenate([x_hi, x_lo], axis=0), w_hi)
        return r[:m] + r[m:] + _dot(x_hi, w_lo)
    return _dot(x_hi, w_hi) + _dot(x_lo, w_hi) + _dot(x_hi, w_lo)


def _mm(x, w_ref, precise):
    if precise:
        return _dot3(x, w_ref[...])
    return _dot(x.astype(BF16), w_ref[...])


def _layer_norm(x, g, b):
    mu = jnp.mean(x, axis=-1, keepdims=True)
    xc = x - mu
    var = jnp.mean(xc * xc, axis=-1, keepdims=True)
    return xc * lax.rsqrt(var + LN_EPS) * g + b


def _silu(x):
    return x * (1.0 / (1.0 + jnp.exp(-x)))


def _gelu_tanh(x):
    return x * (0.5 * (1.0 + jnp.tanh(0.7978845608028654 * (x + 0.044715 * (x * x * x)))))


def _tile_lanes(t, reps):
    return jnp.concatenate([t] * reps, axis=1)


def _rope(x, cos_t, sin_lo, sin_hi):
    reps = x.shape[1] // LANES
    n = x.shape[1]
    nxt = pltpu.roll(x, n - ROT_DIM // 2, axis=1)
    prv = pltpu.roll(x, ROT_DIM // 2, axis=1)
    return (x * _tile_lanes(cos_t, reps) + nxt * _tile_lanes(sin_lo, reps)
            + prv * _tile_lanes(sin_hi, reps))


def _at(block, *lead):
    nd = len(block)
    return pl.BlockSpec((None,) * len(lead) + tuple(block), lambda *_: tuple(lead) + (0,) * nd)


def _full_spec(shape):
    return _at(shape)


def _ada_kernel(c_ref, w_ref, b_ref, o_ref):
    o_ref[...] = _dot3(_silu(c_ref[...]), w_ref[...]) + b_ref[...]


def _ada_call(c_all, ada_w, ada_b):
    depth, d, d6 = ada_w.shape
    r = c_all.shape[0]
    n_mod = d6 // d
    return pl.pallas_call(
        _ada_kernel,
        out_shape=jax.ShapeDtypeStruct((depth, n_mod, r, d), F32),
        grid=(depth, n_mod),
        in_specs=[
            pl.BlockSpec((r, d), lambda l, j: (0, 0)),
            pl.BlockSpec((None, d, d), lambda l, j: (l, 0, j)),
            pl.BlockSpec((None, None, 1, d), lambda l, j: (l, j, 0, 0)),
        ],
        out_specs=pl.BlockSpec((None, None, r, d), lambda l, j: (l, j, 0, 0)),
        compiler_params=_cparams(2),
        name="ada_mod",
    )(c_all, ada_w, ada_b.reshape(depth, n_mod, 1, d))


class _Group:
    def __init__(self, n_rows, tile, rows_per_cond, cond_row0, mods):
        self.n = n_rows
        self.tile = tile
        self.n_tiles = n_rows // tile
        self.per_row = rows_per_cond == 1
        self.precise = self.per_row
        self.mods = mods if self.per_row else mods.reshape(mods.shape[:3] + (1, mods.shape[3]))
        self.tiles_per_cond = None if self.per_row else rows_per_cond // tile
        self.cond_row0 = cond_row0
        self.d = mods.shape[-1]

    def mod_spec(self, layer, j):
        if self.per_row:
            return pl.BlockSpec((None, None, self.tile, self.d), lambda i, *_: (layer, j, 0, 0))
        tpc, r0 = self.tiles_per_cond, self.cond_row0
        return pl.BlockSpec((None, None, None, 1, self.d),
                            lambda i, *_: (layer, j, r0 + i // tpc, 0, 0))

    def row_spec(self, width):
        return pl.BlockSpec((self.tile, width), lambda i, *_: (i, 0))


def _a_mixer_prompt_kernel(x_ref, sh_ref, sc_ref, gt_ref, win_ref, lng_ref, lnb_ref, ws_ref,
                           bst_ref, wout_ref, ng_ref, nb_ref, o_ref):
    x = x_ref[...]
    t = x.shape[0]
    d_a = lng_ref.shape[1]
    h = (x * (1.0 + sc_ref[...]) + sh_ref[...]).astype(BF16)
    z = _gelu_tanh(_dot(h, win_ref[...]))
    u = z[:, :d_a]
    v = _layer_norm(z[:, d_a:], lng_ref[...], lnb_ref[...]).astype(BF16)
    causal = (lax.broadcasted_iota(jnp.int32, (CHUNK, CHUNK), 0)
              >= lax.broadcasted_iota(jnp.int32, (CHUNK, CHUNK), 1))
    gd = d_a // A_GROUPS
    cols = []
    for g in range(A_GROUPS):
        wsg = jnp.where(causal, ws_ref[g], 0.0).astype(BF16)
        bias = bst_ref[:, g:g + 1]
        rows = [_dot(wsg, v[c * CHUNK:(c + 1) * CHUNK, g * gd:(g + 1) * gd]) + bias
                for c in range(t // CHUNK)]
        cols.append(jnp.concatenate(rows, axis=0))
    s = jnp.concatenate(cols, axis=1)
    y = _dot((u * s).astype(BF16), wout_ref[...])
    o_ref[...] = _layer_norm(DEEPNORM_ALPHA * x + (1.0 + gt_ref[...]) * y, ng_ref[...], nb_ref[...])


def _a_mixer_sample_kernel(x_ref, sh_ref, sc_ref, gt_ref, win_ref, lng_ref, lnb_ref, coef_ref,
                           bias_ref, wout_ref, ng_ref, nb_ref, o_ref, v_ref):
    x = x_ref[...]
    d_a = lng_ref.shape[1]
    h = x * (1.0 + sc_ref[...]) + sh_ref[...]
    z = _gelu_tanh(_dot3(h, win_ref[...]))
    u = z[:, :d_a]
    v = _layer_norm(z[:, d_a:], lng_ref[...], lnb_ref[...])
    v_ref[...] = v
    s = v * coef_ref[...] + bias_ref[...]
    y = _dot3(u * s, wout_ref[...])
    o_ref[...] = _layer_norm(DEEPNORM_ALPHA * x + (1.0 + gt_ref[...]) * y, ng_ref[...], nb_ref[...])


def _a_mixer_call(grp, x, layer, p):
    d = x.shape[1]
    d_a = p["a_ln_g"].shape[-1]
    common_in = [
        grp.row_spec(d), grp.mod_spec(layer, 0), grp.mod_spec(layer, 1), grp.mod_spec(layer, 2),
        _at((d, 2 * d_a), layer), _at((1, d_a), layer), _at((1, d_a), layer),
    ]
    tail_in = [_at((d_a, d), layer), _at((1, d), layer, 0), _at((1, d), layer, 0)]
    args_head = (x, grp.mods, grp.mods, grp.mods, p["a_w_in"], p["a_ln_g"], p["a_ln_b"])
    args_tail = (p["a_w_out"], p["ln_g"], p["ln_b"])
    if grp.per_row:
        return pl.pallas_call(
            _a_mixer_sample_kernel,
            out_shape=(jax.ShapeDtypeStruct((grp.n, d), F32), jax.ShapeDtypeStruct((grp.n, d_a), F32)),
            grid=(grp.n_tiles,),
            in_specs=common_in + [_at((1, d_a), layer), _at((1, d_a), layer)] + tail_in,
            out_specs=(grp.row_spec(d), grp.row_spec(d_a)),
            compiler_params=_cparams(1),
            name="a_mixer_sample",
        )(*args_head, p["a_coef"], p["a_bias"], *args_tail)
    out = pl.pallas_call(
        _a_mixer_prompt_kernel,
        out_shape=jax.ShapeDtypeStruct((grp.n, d), F32),
        grid=(grp.n_tiles,),
        in_specs=common_in + [_at((A_GROUPS, CHUNK, CHUNK), layer), _at((CHUNK, A_GROUPS), layer)] + tail_in,
        out_specs=grp.row_spec(d),
        compiler_params=_cparams(1),
        name="a_mixer_prompt",
    )(*args_head, p["a_w_s"], p["a_b_s_t"], *args_tail)
    return out, None


def _top2_of4(a, b, c, d):
    hi1, lo1 = jnp.maximum(a, b), jnp.minimum(a, b)
    hi2, lo2 = jnp.maximum(c, d), jnp.minimum(c, d)
    return jnp.maximum(hi1, hi2) + jnp.maximum(jnp.minimum(hi1, hi2), jnp.maximum(lo1, lo2))


def _first_argmax4(vals):
    m = jnp.maximum(jnp.maximum(vals[0], vals[1]), jnp.maximum(vals[2], vals[3]))
    idx = jnp.where(vals[0] == m, 0, jnp.where(vals[1] == m, 1, jnp.where(vals[2] == m, 2, 3)))
    return m, idx


def _router_kernel(x_ref, sh_ref, sc_ref, rwt_ref, rb_ref, hr_ref, gid_ref, rank_ref, cnt_ref,
                   carry_ref):
    i = pl.program_id(0)
    t, d = x_ref.shape

    @pl.when(i == 0)
    def _():
        carry_ref[...] = jnp.zeros_like(carry_ref)

    h = x_ref[...] * (1.0 + sc_ref[...]) + sh_ref[...]
    logits = lax.dot_general(rwt_ref[...], h, (((1,), (1,)), ((), ())),
                             precision=lax.Precision.HIGHEST, preferred_element_type=F32)
    mx = jnp.max(logits, axis=0, keepdims=True)
    ex = jnp.exp(logits - mx)
    scores = ex / jnp.sum(ex, axis=0, keepdims=True)
    sel = scores + rb_ref[...]
    sel_r = [sel[e:e + 1, :] for e in range(N_EXPERTS)]
    sc_r = [scores[e:e + 1, :] for e in range(N_EXPERTS)]
    epg = EXPERTS_PER_GROUP
    gs = [_top2_of4(*sel_r[g * epg:(g + 1) * epg]) for g in range(N_EXPERT_GROUPS)]
    best, gid = gs[0], jnp.zeros((1, t), jnp.int32)
    for g in range(1, N_EXPERT_GROUPS):
        upd = gs[g] > best
        gid = jnp.where(upd, g, gid)
        best = jnp.where(upd, gs[g], best)
    cand_sel, cand_sc = [], []
    for k in range(epg):
        a, b = sel_r[k], sc_r[k]
        for g in range(1, N_EXPERT_GROUPS):
            a = jnp.where(gid == g, sel_r[g * epg + k], a)
            b = jnp.where(gid == g, sc_r[g * epg + k], b)
        cand_sel.append(a)
        cand_sc.append(b)
    _, i1 = _first_argmax4(cand_sel)
    rest = [jnp.where(i1 == k, -jnp.inf, cand_sel[k]) for k in range(epg)]
    _, i2 = _first_argmax4(rest)
    w1 = sum(jnp.where(i1 == k, cand_sc[k], 0.0) for k in range(epg))
    w2 = sum(jnp.where(i2 == k, cand_sc[k], 0.0) for k in range(epg))
    tot = w1 + w2
    w1, w2 = w1 / tot, w2 / tot
    gate_rows = [jnp.where(i1 == k, w1, 0.0) + jnp.where(i2 == k, w2, 0.0) for k in range(epg)]
    gates_t = jnp.concatenate(gate_rows + [jnp.zeros((GATE_LANES - epg, t), F32)], axis=0)
    hr_ref[:, :d] = h
    hr_ref[:, d:] = gates_t.T
    onehot = jnp.concatenate(
        [(gid == g).astype(F32) for g in range(N_EXPERT_GROUPS)]
        + [jnp.zeros((SUBLANES - N_EXPERT_GROUPS, t), F32)], axis=0)
    before = (lax.broadcasted_iota(jnp.int32, (t, t), 0)
              < lax.broadcasted_iota(jnp.int32, (t, t), 1)).astype(BF16)
    rank_in_tile = _dot(onehot.astype(BF16), before)
    carry = carry_ref[...]
    rank = jnp.sum(onehot * (rank_in_tile + carry[:, :1]), axis=0, keepdims=True)
    gid_ref[...] = gid
    rank_ref[...] = rank.astype(jnp.int32)
    carry = carry + jnp.sum(onehot, axis=1, keepdims=True)
    carry_ref[...] = carry
    cnt_ref[...] = carry.astype(jnp.int32)


def _router_call(grp, x, layer, p):
    d = x.shape[1]
    per_token = pl.BlockSpec((1, grp.tile), lambda i: (0, i))
    return pl.pallas_call(
        _router_kernel,
        out_shape=(jax.ShapeDtypeStruct((grp.n, d + GATE_LANES), F32),
                   jax.ShapeDtypeStruct((1, grp.n), jnp.int32),
                   jax.ShapeDtypeStruct((1, grp.n), jnp.int32),
                   jax.ShapeDtypeStruct((SUBLANES, LANES), jnp.int32)),
        grid=(grp.n_tiles,),
        in_specs=[grp.row_spec(d), grp.mod_spec(layer, 3), grp.mod_spec(layer, 4),
                  _full_spec((N_EXPERTS, d)), _full_spec((N_EXPERTS, 1))],
        out_specs=(grp.row_spec(d + GATE_LANES), per_token, per_token, _full_spec((SUBLANES, LANES))),
        scratch_shapes=[pltpu.VMEM((SUBLANES, LANES), F32)],
        compiler_params=_cparams(1),
        name="moe_router",
    )(x, grp.mods, grp.mods, p["router_wt"], p["router_b"])


def _row_copy(src_ref, src_row, dst_ref, dst_row, sem):
    return pltpu.make_async_copy(src_ref.at[pl.ds(src_row, 1), :], dst_ref.at[pl.ds(dst_row, 1), :], sem)


def _scatter_kernel(pos_ref, gap_ref, hr_ref, out_ref, zero_ref, sem, zsem, *, tile):
    i = pl.program_id(0)
    base = i * tile

    def issue(r, c):
        _row_copy(hr_ref, r, out_ref, pos_ref[base + r], sem).start()
        return c

    lax.fori_loop(0, tile, issue, 0, unroll=DMA_ISSUE_UNROLL)

    @pl.when(i == pl.num_programs(0) - 1)
    def _():
        zero_ref[...] = jnp.zeros_like(zero_ref)
        for g in range(N_EXPERT_GROUPS):
            lo, hi = gap_ref[g], gap_ref[N_EXPERT_GROUPS + g]

            def zissue(r, c):
                _row_copy(zero_ref, 0, out_ref, r, zsem).start()
                return c

            def zwait(r, c):
                _row_copy(zero_ref, 0, out_ref, r, zsem).wait()
                return c

            lax.fori_loop(lo, hi, zissue, 0)
            lax.fori_loop(lo, hi, zwait, 0)

    pltpu.make_async_copy(hr_ref, out_ref.at[pl.ds(0, tile), :], sem).wait()


def _scatter_call(grp, hr, pos, gaps, n_rows_out):
    w = hr.shape[1]
    grid_spec = pltpu.PrefetchScalarGridSpec(
        num_scalar_prefetch=2,
        grid=(grp.n_tiles,),
        in_specs=[pl.BlockSpec((grp.tile, w), lambda i, *_: (i, 0))],
        out_specs=pl.BlockSpec(memory_space=pl.ANY),
        scratch_shapes=[pltpu.VMEM((SUBLANES, w), F32), pltpu.SemaphoreType.DMA(()),
                        pltpu.SemaphoreType.DMA(())],
    )
    return pl.pallas_call(
        functools.partial(_scatter_kernel, tile=grp.tile),
        out_shape=jax.ShapeDtypeStruct((n_rows_out, w), F32),
        grid_spec=grid_spec,
        compiler_params=_cparams(1),
        name="moe_scatter",
    )(pos, gaps, hr)


def _experts_kernel(grp_ref, na_ref, rows_ref, wg_ref, wu_ref, wd_ref, o_ref):
    j = pl.program_id(0)

    @pl.when(j >= na_ref[0])
    def _():
        o_ref[...] = jnp.zeros_like(o_ref)

    @pl.when(j < na_ref[0])
    def _():
        d = o_ref.shape[1]
        hb = rows_ref[:, :d].astype(BF16)
        acts = []
        for e in range(EXPERTS_PER_GROUP):
            g = _dot(hb, wg_ref[e])
            u = _dot(hb, wu_ref[e])
            acts.append((_silu(g) * u * rows_ref[:, d + e:d + e + 1]).astype(BF16))
        a = jnp.concatenate(acts, axis=1)
        f = wd_ref.shape[1]
        o_ref[...] = _dot(a, wd_ref[...].reshape(EXPERTS_PER_GROUP * f, d))


def _experts_call(rows, tile_grp, n_active, layer, p, expert_tile):
    n_rows, w = rows.shape
    _, _, epg, d, f = p["moe_w_gate"].shape
    grid_spec = pltpu.PrefetchScalarGridSpec(
        num_scalar_prefetch=2,
        grid=(n_rows // expert_tile,),
        in_specs=[
            pl.BlockSpec((expert_tile, w), lambda j, g, na: (jnp.minimum(j, na[0] - 1), 0)),
            pl.BlockSpec((None, None, epg, d, f), lambda j, g, na: (layer, g[j], 0, 0, 0)),
            pl.BlockSpec((None, None, epg, d, f), lambda j, g, na: (layer, g[j], 0, 0, 0)),
            pl.BlockSpec((None, None, epg, f, d), lambda j, g, na: (layer, g[j], 0, 0, 0)),
        ],
        out_specs=pl.BlockSpec((expert_tile, d), lambda j, g, na: (j, 0)),
    )
    return pl.pallas_call(
        _experts_kernel,
        out_shape=jax.ShapeDtypeStruct((n_rows, d), F32),
        grid_spec=grid_spec,
        compiler_params=_cparams(1),
        name="moe_experts",
    )(tile_grp, n_active, rows, p["moe_w_gate"], p["moe_w_up"], p["moe_w_down"])


def _experts_precise_kernel(grp_ref, na_ref, rows_ref, wg_ref, wu_ref, wd_ref, o_ref):
    j = pl.program_id(0)
    e = pl.program_id(1)

    @pl.when(e == 0)
    def _():
        o_ref[...] = jnp.zeros_like(o_ref)

    @pl.when(j < na_ref[0])
    def _():
        d = o_ref.shape[1]
        h = rows_ref[:, :d]
        gate = sum(jnp.where(e == k, rows_ref[:, d + k:d + k + 1], 0.0) for k in range(EXPERTS_PER_GROUP))
        a = _silu(_dot3(h, wg_ref[...])) * _dot3(h, wu_ref[...]) * gate
        o_ref[...] += _dot3(a, wd_ref[...])


def _experts_precise_call(rows, tile_grp, n_active, layer, p, expert_tile):
    n_rows, w = rows.shape
    _, _, epg, d, f = p["moe_w_gate"].shape

    def w_index(j, k, g, na):
        return (layer, g[j], jnp.where(j < na[0], k, epg - 1), 0, 0)

    grid_spec = pltpu.PrefetchScalarGridSpec(
        num_scalar_prefetch=2,
        grid=(n_rows // expert_tile, epg),
        in_specs=[
            pl.BlockSpec((expert_tile, w), lambda j, k, g, na: (jnp.minimum(j, na[0] - 1), 0)),
            pl.BlockSpec((None, None, None, d, f), w_index),
            pl.BlockSpec((None, None, None, d, f), w_index),
            pl.BlockSpec((None, None, None, f, d), w_index),
        ],
        out_specs=pl.BlockSpec((expert_tile, d), lambda j, k, g, na: (j, 0)),
    )
    return pl.pallas_call(
        _experts_precise_kernel,
        out_shape=jax.ShapeDtypeStruct((n_rows, d), F32),
        grid_spec=grid_spec,
        compiler_params=_cparams(2),
        name="moe_experts_precise",
    )(tile_grp, n_active, rows, p["moe_w_gate"], p["moe_w_up"], p["moe_w_down"])


def _row_plan(counts, gid, rank, expert_tile, n_tiles):
    nt = (counts + expert_tile - 1) // expert_tile
    cum = jnp.cumsum(nt)
    starts = (cum - nt) * expert_tile
    pos = starts[gid] + rank
    gap_lo = starts + counts
    gap_hi = jnp.concatenate([starts[1:], jnp.full((1,), n_tiles * expert_tile, jnp.int32)])
    n_active = cum[-1]
    j = jnp.minimum(jnp.arange(n_tiles, dtype=jnp.int32), n_active - 1)
    tile_grp = jnp.sum((j[:, None] >= cum[None, :]).astype(jnp.int32), axis=1)
    gaps = jnp.concatenate([gap_lo, gap_hi]).astype(jnp.int32)
    return pos.astype(jnp.int32), gaps, tile_grp.astype(jnp.int32), n_active.reshape(1).astype(jnp.int32)


def _moe_post_kernel(pos_ref, x_ref, gt_ref, ng_ref, nb_ref, f_ref, o_ref, buf_ref, sem, *, tile):
    i = pl.program_id(0)
    n = pl.num_programs(0)

    def issue(step, slot):
        base = step * tile

        def body(r, c):
            _row_copy(f_ref, pos_ref[base + r], buf_ref.at[slot], r, sem.at[slot]).start()
            return c

        lax.fori_loop(0, tile, body, 0, unroll=DMA_ISSUE_UNROLL)

    @pl.when(i == 0)
    def _():
        issue(0, 0)

    slot = i % 2

    @pl.when(i + 1 < n)
    def _():
        issue(i + 1, 1 - slot)

    pltpu.make_async_copy(f_ref.at[pl.ds(0, tile), :], buf_ref.at[slot], sem.at[slot]).wait()
    x = x_ref[...]
    o_ref[...] = _layer_norm(DEEPNORM_ALPHA * x + (1.0 + gt_ref[...]) * buf_ref[slot],
                             ng_ref[...], nb_ref[...])


def _moe_post_call(grp, x, layer, f_rows, pos, p):
    d = x.shape[1]
    grid_spec = pltpu.PrefetchScalarGridSpec(
        num_scalar_prefetch=1,
        grid=(grp.n_tiles,),
        in_specs=[grp.row_spec(d), grp.mod_spec(layer, 5), _at((1, d), layer, 1), _at((1, d), layer, 1),
                  pl.BlockSpec(memory_space=pl.ANY)],
        out_specs=grp.row_spec(d),
        scratch_shapes=[pltpu.VMEM((2, grp.tile, d), F32), pltpu.SemaphoreType.DMA((2,))],
    )
    return pl.pallas_call(
        functools.partial(_moe_post_kernel, tile=grp.tile),
        out_shape=jax.ShapeDtypeStruct((grp.n, d), F32),
        grid_spec=grid_spec,
        compiler_params=_cparams(1),
        name="moe_post",
    )(pos, x, grp.mods, p["ln_g"], p["ln_b"], f_rows)


def _moe_sublayer(grp, x, layer, p, expert_tile):
    hr, gid, rank, cnt = _router_call(grp, x, layer, p)
    n_tiles = grp.n // expert_tile + N_EXPERT_GROUPS
    pos, gaps, tile_grp, n_active = _row_plan(cnt[:N_EXPERT_GROUPS, 0], gid.reshape(grp.n),
                                              rank.reshape(grp.n), expert_tile, n_tiles)
    rows = _scatter_call(grp, hr, pos, gaps, n_tiles * expert_tile)
    experts = _experts_precise_call if grp.precise else _experts_call
    f_rows = experts(rows, tile_grp, n_active, layer, p, expert_tile)
    return _moe_post_call(grp, x, layer, f_rows, pos, p)


def _kv_kernel(x_ref, wk_ref, wv_ref, cos_ref, slo_ref, shi_ref, *out_refs, precise):
    x = x_ref[...]
    k = _rope(_mm(x, wk_ref, precise), cos_ref[...], slo_ref[...], shi_ref[...])
    v = _mm(x, wv_ref, precise)
    out_refs[0][...] = k
    out_refs[1][...] = v
    if len(out_refs) == 4:
        out_refs[2][...] = k.astype(BF16)
        out_refs[3][...] = v.astype(BF16)


def _rope_spec(grp, tables_rows):
    if tables_rows == 1:
        return _full_spec((1, LANES))
    tiles = tables_rows // grp.tile
    return pl.BlockSpec((grp.tile, LANES), lambda i, *_: (i % tiles, 0))


def _kv_call(grp, x, p, rope_tabs, with_bf16):
    d = x.shape[1]
    wk_cols, wv_cols = p["kv_w_k"].shape[1], p["kv_w_v"].shape[1]
    rs = _rope_spec(grp, rope_tabs[0].shape[0])
    out_shape = [jax.ShapeDtypeStruct((grp.n, wk_cols), F32), jax.ShapeDtypeStruct((grp.n, wv_cols), F32)]
    out_specs = [grp.row_spec(wk_cols), grp.row_spec(wv_cols)]
    if with_bf16:
        out_shape += [jax.ShapeDtypeStruct((grp.n, wk_cols), BF16), jax.ShapeDtypeStruct((grp.n, wv_cols), BF16)]
        out_specs += [grp.row_spec(wk_cols), grp.row_spec(wv_cols)]
    return pl.pallas_call(
        functools.partial(_kv_kernel, precise=grp.precise),
        out_shape=tuple(out_shape),
        grid=(grp.n_tiles,),
        in_specs=[grp.row_spec(d), _full_spec((d, wk_cols)), _full_spec((d, wv_cols)), rs, rs, rs],
        out_specs=tuple(out_specs),
        compiler_params=_cparams(1),
        name="kv_proj",
    )(x, p["kv_w_k"], p["kv_w_v"], *rope_tabs)


def _q_kernel(x_ref, sh_ref, sc_ref, wq_ref, cos_ref, slo_ref, shi_ref, o_ref, *, precise):
    h = x_ref[...] * (1.0 + sc_ref[...]) + sh_ref[...]
    q = _rope(_mm(h, wq_ref, precise), cos_ref[...], slo_ref[...], shi_ref[...])
    o_ref[...] = (q * (HEAD_DIM ** -0.5)).astype(o_ref.dtype)


def _q_call(grp, x, layer, p, rope_tabs, out_dtype):
    d = x.shape[1]
    j = layer - N_A_LAYERS
    cols = p["b_w_q"].shape[-1]
    rs = _rope_spec(grp, rope_tabs[0].shape[0])
    return pl.pallas_call(
        functools.partial(_q_kernel, precise=grp.precise),
        out_shape=jax.ShapeDtypeStruct((grp.n, cols), out_dtype),
        grid=(grp.n_tiles,),
        in_specs=[grp.row_spec(d), grp.mod_spec(layer, 0), grp.mod_spec(layer, 1),
                  _at((d, cols), j), rs, rs, rs],
        out_specs=grp.row_spec(cols),
        compiler_params=_cparams(1),
        name="q_proj",
    )(x, grp.mods, grp.mods, p["b_w_q"], *rope_tabs)


def _diff_lambda(lamv_ref, lam_init):
    lv = lamv_ref[...]
    a = jnp.exp(jnp.sum(lv[0:1] * lv[1:2], axis=1, keepdims=True))
    b = jnp.exp(jnp.sum(lv[2:3] * lv[3:4], axis=1, keepdims=True))
    return a - b + lam_init


def _attn_prompt_kernel(q_ref, k_ref, v_ref, lamv_ref, sg_ref, o_ref, s_ref, p_ref, m_ref, acc_ref,
                        *, lam_init, t):
    i = pl.program_id(2)
    q = q_ref[...]
    lane = lax.broadcasted_iota(jnp.int32, q.shape, 1)
    zero = jnp.zeros_like(q)
    qq = jnp.concatenate([jnp.where(lane < HEAD_DIM, q, zero), jnp.where(lane >= HEAD_DIM, q, zero)], axis=0)
    n_chunks = 2 * t // SOFTMAX_ROWS
    halves = t // LANES

    def scores(j, masked):
        start = pl.multiple_of(j * t, t)
        s = _dot_nt(qq, k_ref[pl.ds(start, t), :])
        if masked:
            r = lax.broadcasted_iota(jnp.int32, s.shape, 0)
            r = jnp.where(r >= t, r - t, r)
            c = lax.broadcasted_iota(jnp.int32, s.shape, 1)
            s = jnp.where(c <= r, s, NEG_BIG)
        s_ref[j] = s
        part = s[:, :LANES]
        for hh in range(1, halves):
            part = jnp.maximum(part, s[:, hh * LANES:(hh + 1) * LANES])
        m_ref[...] = jnp.maximum(m_ref[...], part)

    m_ref[...] = jnp.full(m_ref.shape, NEG_BIG, F32)

    def pass1(j, c):
        scores(j, False)
        return c

    lax.fori_loop(0, i, pass1, 0)
    scores(i, True)
    m_ref[...] = jnp.broadcast_to(jnp.max(m_ref[...], axis=1, keepdims=True), m_ref.shape)
    acc_ref[...] = jnp.zeros_like(acc_ref)
    ones = jnp.ones((t, LANES), BF16)

    def pass2(j, c):
        for ch in range(n_chunks):
            rows = slice(ch * SOFTMAX_ROWS, (ch + 1) * SOFTMAX_ROWS)
            m = m_ref[rows, :]
            sc = s_ref[j, rows, :]
            pieces = [jnp.exp(sc[:, hh * LANES:(hh + 1) * LANES] - m) for hh in range(halves)]
            p_ref[rows, :] = jnp.concatenate(pieces, axis=1).astype(BF16)
        start = pl.multiple_of(j * t, t)
        v1 = jnp.concatenate([v_ref[pl.ds(start, t), :], ones], axis=1)
        acc_ref[...] += _dot(p_ref[...], v1)
        return c

    lax.fori_loop(0, i + 1, pass2, 0)
    acc = acc_ref[...]
    o = acc[:, :V_DIM] / acc[:, V_DIM:]
    od = o[:t] - _diff_lambda(lamv_ref, lam_init) * o[t:]
    ms = jnp.mean(od * od, axis=-1, keepdims=True)
    o_ref[...] = (od * lax.rsqrt(ms + LN_EPS) * sg_ref[...] * (1.0 - lam_init)).astype(o_ref.dtype)


def _attn_prompt_call(q, k, v, layer, p, lam_init, batch, seq, t):
    n, cols = q.shape
    nq = seq // t
    j = layer - N_A_LAYERS
    return pl.pallas_call(
        functools.partial(_attn_prompt_kernel, lam_init=lam_init, t=t),
        out_shape=jax.ShapeDtypeStruct((n, cols), BF16),
        grid=(batch, N_HEADS, nq),
        in_specs=[
            pl.BlockSpec((t, V_DIM), lambda b, h, i: (b * nq + i, h)),
            pl.BlockSpec((seq, V_DIM), lambda b, h, i: (b, h)),
            pl.BlockSpec((seq, V_DIM), lambda b, h, i: (b, h)),
            _at((4, HEAD_DIM), j),
            _at((1, V_DIM), j),
        ],
        out_specs=pl.BlockSpec((t, V_DIM), lambda b, h, i: (b * nq + i, h)),
        scratch_shapes=[pltpu.VMEM((nq, 2 * t, t), F32), pltpu.VMEM((2 * t, t), BF16),
                        pltpu.VMEM((2 * t, LANES), F32), pltpu.VMEM((2 * t, 2 * V_DIM), F32)],
        compiler_params=_cparams(3),
        name="attn_prompt",
    )(q, k, v, p["lamv"], p["subln_g"])


def _attn_sample_kernel(pt_ref, q_ref, kn_ref, vn_ref, lamv_ref, sg_ref, *rest, lam_init):
    del pt_ref
    npg = PAGES_PER_STEP
    k_refs, v_refs = rest[:npg], rest[npg:2 * npg]
    o_ref, qcol_ref, m_ref, l_ref, acc_ref = rest[2 * npg:]
    step = pl.program_id(1)
    n_rows = 2 * N_HEADS
    cols = q_ref.shape[1]

    def seg_sums(prod):
        return jnp.sum(prod.reshape(n_rows, HEAD_DIM, prod.shape[1]), axis=1)

    def as_column(row_vec):
        return jnp.transpose(jnp.broadcast_to(row_vec, (LANES, cols)))

    @pl.when(step == 0)
    def _():
        qcol = as_column(q_ref[...])
        qcol_ref[...] = qcol
        m_ref[...] = seg_sums(qcol * as_column(kn_ref[...]))[:, :1]
        l_ref[...] = jnp.ones_like(l_ref)
        vn = vn_ref[...]
        acc_ref[...] = jnp.concatenate(
            [jnp.broadcast_to(vn[:, h * V_DIM:(h + 1) * V_DIM], (2, V_DIM)) for h in range(N_HEADS)], axis=0)

    qcol = qcol_ref[...]
    s = jnp.concatenate([seg_sums(r[...] * qcol) for r in k_refs], axis=1)
    m_old = m_ref[...]
    m_new = jnp.maximum(m_old, jnp.max(s, axis=1, keepdims=True))
    a = jnp.exp(m_old - m_new)
    p = jnp.exp(s - m_new)
    l_ref[...] = a * l_ref[...] + jnp.sum(p, axis=1, keepdims=True)
    m_ref[...] = m_new
    p_hi, p_lo = _split_bf16(p)
    prow = lax.broadcasted_iota(jnp.int32, p.shape, 0)
    zero = jnp.zeros((), BF16)
    pv = jnp.zeros(acc_ref.shape, F32)
    for h in range(N_HEADS):
        vh = jnp.concatenate([r[pl.ds(h, PAGE_SIZE, stride=N_HEADS), :] for r in v_refs], axis=0)
        v_hi, v_lo = _split_bf16(vh)
        mine = (prow // 2) == h
        ph_hi = jnp.where(mine, p_hi, zero)
        r = _dot(jnp.concatenate([ph_hi, jnp.where(mine, p_lo, zero)], axis=0), v_hi)
        pv = pv + r[:n_rows] + r[n_rows:] + _dot(ph_hi, v_lo)
    acc_ref[...] = a * acc_ref[...] + pv

    @pl.when(step == pl.num_programs(1) - 1)
    def _():
        o = acc_ref[...] / l_ref[...]
        lam = _diff_lambda(lamv_ref, lam_init)
        outs = []
        for h in range(N_HEADS):
            od = o[2 * h:2 * h + 1] - lam * o[2 * h + 1:2 * h + 2]
            ms = jnp.mean(od * od, axis=-1, keepdims=True)
            outs.append(od * lax.rsqrt(ms + LN_EPS) * sg_ref[...] * (1.0 - lam_init))
        o_ref[...] = jnp.concatenate(outs, axis=1)


def _attn_sample_call(q, k_new, v_new, cache_kt, cache_v2, page_table, layer, p, lam_init):
    n_seq, cols = q.shape
    n_pages = page_table.shape[1]
    npg = PAGES_PER_STEP
    j = layer - N_A_LAYERS
    row3 = lambda a: a.reshape(n_seq, 1, cols)
    row_spec = pl.BlockSpec((None, 1, cols), lambda b, g, pt: (b, 0, 0))

    def page_spec(i, shape):
        return pl.BlockSpec((None,) + shape, lambda b, g, pt: (pt[b * n_pages + g * npg + i], 0, 0))

    grid_spec = pltpu.PrefetchScalarGridSpec(
        num_scalar_prefetch=1,
        grid=(n_seq, n_pages // npg),
        in_specs=[row_spec, row_spec, row_spec, _at((4, HEAD_DIM), j), _at((1, V_DIM), j)]
        + [page_spec(i, cache_kt.shape[1:]) for i in range(npg)]
        + [page_spec(i, cache_v2.shape[1:]) for i in range(npg)],
        out_specs=row_spec,
        scratch_shapes=[pltpu.VMEM((cols, LANES), F32), pltpu.VMEM((2 * N_HEADS, 1), F32),
                        pltpu.VMEM((2 * N_HEADS, 1), F32), pltpu.VMEM((2 * N_HEADS, V_DIM), F32)],
    )
    out = pl.pallas_call(
        functools.partial(_attn_sample_kernel, lam_init=lam_init),
        out_shape=jax.ShapeDtypeStruct((n_seq, 1, cols), F32),
        grid_spec=grid_spec,
        compiler_params=_cparams(2),
        name="attn_sample",
    )(page_table.reshape(-1), row3(q), row3(k_new), row3(v_new), p["lamv"], p["subln_g"],
      *([cache_kt] * npg), *([cache_v2] * npg))
    return out.reshape(n_seq, cols)


def _attn_out_kernel(o_ref, x_ref, gt_ref, wo_ref, ng_ref, nb_ref, y_ref, *, precise):
    y = _mm(o_ref[...], wo_ref, precise)
    x = x_ref[...]
    y_ref[...] = _layer_norm(DEEPNORM_ALPHA * x + (1.0 + gt_ref[...]) * y, ng_ref[...], nb_ref[...])


def _attn_out_call(grp, o, x, layer, p):
    d = x.shape[1]
    cols = o.shape[1]
    j = layer - N_A_LAYERS
    return pl.pallas_call(
        functools.partial(_attn_out_kernel, precise=grp.precise),
        out_shape=jax.ShapeDtypeStruct((grp.n, d), F32),
        grid=(grp.n_tiles,),
        in_specs=[grp.row_spec(cols), grp.row_spec(d), grp.mod_spec(layer, 2), _at((cols, d), j),
                  _at((1, d), layer, 0), _at((1, d), layer, 0)],
        out_specs=grp.row_spec(d),
        compiler_params=_cparams(1),
        name="attn_out",
    )(o, x, grp.mods, p["b_w_o"], p["ln_g"], p["ln_b"])


def _rope_tables(pos):
    half = ROT_DIM // 2
    inv = 1.0 / (ROPE_THETA ** (jnp.arange(0, ROT_DIM, 2, dtype=F32) / ROT_DIM))
    ang = pos.astype(F32)[:, None] * inv[None, :]
    cos, sin = jnp.cos(ang), jnp.sin(ang)
    n = pos.shape[0]
    ones = jnp.ones((n, HEAD_DIM - ROT_DIM), F32)
    zeros = jnp.zeros((n, HEAD_DIM - ROT_DIM), F32)
    zh = jnp.zeros((n, half), F32)
    cos_t = jnp.concatenate([cos, cos, ones], axis=1)
    sin_lo = jnp.concatenate([-sin, zh, zeros], axis=1)
    sin_hi = jnp.concatenate([zh, sin, zeros], axis=1)
    rep = LANES // HEAD_DIM
    return tuple(jnp.tile(t, (1, rep)) for t in (cos_t, sin_lo, sin_hi))


def _trunk(grp, x, p, rope_tabs, attend, expert_tile, want_bf16_kv):
    v_rows = []
    k_sh = v_sh = kv_attn = None
    for l in range(DEPTH):
        if l < N_A_LAYERS:
            x, v_a = _a_mixer_call(grp, x, l, p)
            v_rows.append(v_a)
        else:
            lam_init = 0.8 - 0.6 * math.exp(-0.3 * l)
            q = _q_call(grp, x, l, p, rope_tabs, BF16 if want_bf16_kv else F32)
            o = attend(q, kv_attn, l, p, lam_init)
            x = _attn_out_call(grp, o, x, l, p)
        x = _moe_sublayer(grp, x, l, p, expert_tile)
        if l == N_A_LAYERS - 1:
            outs = _kv_call(grp, x, p, rope_tabs, want_bf16_kv)
            k_sh, v_sh = outs[0], outs[1]
            kv_attn = (outs[2], outs[3]) if want_bf16_kv else (k_sh, v_sh)
    return x, k_sh, v_sh, v_rows


def kernel(x_prompt, x_sample, cache_k, cache_v, page_table, c_prompt, c_sample, ada_w, ada_b, ln_g, ln_b, a_w_in, a_ln_g, a_ln_b, a_w_s, a_b_s, a_w_out, kv_w_k, kv_w_v, b_w_q, b_lam_q1, b_lam_k1, b_lam_q2, b_lam_k2, b_subln_g, b_w_o, router_w, router_bias, moe_w_gate, moe_w_up, moe_w_down):
    batch, seq, d = x_prompt.shape
    dec_batch, dec_seq, _ = x_sample.shape
    assert dec_seq == 1, "the sample group is one new token per sequence"
    n_pool = cache_k.shape[0]
    past = page_table.shape[1] * PAGE_SIZE
    depth, n_exp, _, d_exp = moe_w_gate.shape
    n_a, d_a = a_ln_g.shape
    n_b = b_w_q.shape[0]
    gd = d_a // A_GROUPS
    epg = EXPERTS_PER_GROUP

    p_f32 = dict(
        a_w_in=a_w_in, a_w_out=a_w_out, kv_w_k=kv_w_k, kv_w_v=kv_w_v, b_w_q=b_w_q, b_w_o=b_w_o,
        moe_w_gate=moe_w_gate.reshape(depth, n_exp // epg, epg, d, d_exp),
        moe_w_up=moe_w_up.reshape(depth, n_exp // epg, epg, d, d_exp),
        moe_w_down=moe_w_down.reshape(depth, n_exp // epg, epg, d_exp, d),
        a_ln_g=a_ln_g.reshape(n_a, 1, d_a), a_ln_b=a_ln_b.reshape(n_a, 1, d_a),
        a_w_s=a_w_s, a_b_s_t=jnp.swapaxes(a_b_s, 1, 2),
        a_coef=jnp.repeat(a_w_s[:, :, 0, 0], gd, axis=1).reshape(n_a, 1, d_a),
        a_bias=jnp.repeat(a_b_s[:, :, 0], gd, axis=1).reshape(n_a, 1, d_a),
        ln_g=ln_g.reshape(depth, 2, 1, d), ln_b=ln_b.reshape(depth, 2, 1, d),
        lamv=jnp.stack([b_lam_q1, b_lam_k1, b_lam_q2, b_lam_k2], axis=1),
        subln_g=b_subln_g.reshape(n_b, 1, V_DIM),
        router_wt=router_w.T, router_b=router_bias.reshape(N_EXPERTS, 1),
    )
    matmul_weights = ("a_w_in", "a_w_out", "kv_w_k", "kv_w_v", "b_w_q", "b_w_o",
                      "moe_w_gate", "moe_w_up", "moe_w_down")
    p_bf16 = {k: (v.astype(BF16) if k in matmul_weights else v) for k, v in p_f32.items()}

    mods = _ada_call(jnp.concatenate([c_sample, c_prompt], axis=0), ada_w, ada_b)

    gp = _Group(batch * seq, PROMPT_TILE, seq, dec_batch, mods)
    tabs_p = _rope_tables(jnp.arange(seq, dtype=jnp.int32))

    def attend_prompt(q, kv, layer, p, lam_init):
        return _attn_prompt_call(q, kv[0], kv[1], layer, p, lam_init, batch, seq, ATTN_TILE)

    y_p, k_p, v_p, _ = _trunk(gp, x_prompt.reshape(batch * seq, d), p_bf16, tabs_p, attend_prompt,
                              PROMPT_EXPERT_TILE, True)

    gs = _Group(dec_batch, dec_batch, 1, 0, mods)
    tabs_s = _rope_tables(jnp.full((1,), past, dtype=jnp.int32))
    cache_kt = jnp.transpose(cache_k, (0, 2, 3, 4, 1)).reshape(n_pool, N_HEADS * 2 * HEAD_DIM, PAGE_SIZE)
    cache_v2 = cache_v.reshape(n_pool, PAGE_SIZE * N_HEADS, V_DIM)

    def attend_sample(q, kv, layer, p, lam_init):
        return _attn_sample_call(q, kv[0], kv[1], cache_kt, cache_v2, page_table, layer, p, lam_init)

    y_s, k_s, v_s, v_rows = _trunk(gs, x_sample.reshape(dec_batch, d), p_f32, tabs_s, attend_sample,
                                   dec_batch, False)

    return (
        y_p.reshape(batch, seq, d),
        y_s.reshape(dec_batch, 1, d),
        k_p.reshape(batch, seq, N_HEADS, 2, HEAD_DIM),
        v_p.reshape(batch, seq, N_HEADS, V_DIM),
        k_s.reshape(dec_batch, 1, N_HEADS, 2, HEAD_DIM),
        v_s.reshape(dec_batch, 1, N_HEADS, V_DIM),
        jnp.stack(v_rows, axis=0).reshape(N_A_LAYERS, dec_batch, 1, d_a),
    )
```

```python
import functools
import math

import jax
import jax.numpy as jnp
from jax import lax
from jax.experimental import pallas as pl
from jax.experimental.pallas import tpu as pltpu

DEPTH = 4
N_A_LAYERS = DEPTH // 2
CHUNK = 128
A_GROUPS = 8
N_HEADS = 8
HEAD_DIM = 64
V_DIM = 2 * HEAD_DIM
ROT_DIM = HEAD_DIM // 4
ROPE_THETA = 500000.0
N_EXPERTS = 16
N_EXPERT_GROUPS = 4
EXPERTS_PER_GROUP = N_EXPERTS // N_EXPERT_GROUPS
LN_EPS = 1e-5
DEEPNORM_ALPHA = (2 * DEPTH) ** 0.25
PAGE_SIZE = 128

LANES = 128
SUBLANES = 8
BF16_SUBLANES = 16
VMEM_LIMIT_BYTES = 56 * 1024 * 1024

GATE_LANES = LANES
NEG_BIG = -1e30
PAGES_PER_STEP = 8
DMA_ISSUE_UNROLL = 8
SOFTMAX_ROWS = 32

PROMPT_TILE = 512
PROMPT_EXPERT_TILE = 512
ATTN_TILE = 256

F32 = jnp.float32
BF16 = jnp.bfloat16


def _cparams(n_axes):
    return pltpu.CompilerParams(
        dimension_semantics=("arbitrary",) * n_axes, vmem_limit_bytes=VMEM_LIMIT_BYTES)


def _dot(a, b):
    return jnp.dot(a, b, preferred_element_type=F32)


def _dot_nt(a, b):
    return lax.dot_general(a, b, (((1,), (1,)), ((), ())), preferred_element_type=F32)


def _split_bf16(a):
    hi = a.astype(BF16)
    return hi, (a - hi.astype(F32)).astype(BF16)


def _dot3(x, w):
    x_hi, x_lo = _split_bf16(x)
    w_hi, w_lo = _split_bf16(w)
    m = x.shape[0]
    if m % BF16_SUBLANES == 0:
        r = _dot(jnp.concatenate([x_hi, x_lo], axis=0), w_hi)
        return r[:m] + r[m:] + _dot(x_hi, w_lo)
    return _dot(x_hi, w_hi) + _dot(x_lo, w_hi) + _dot(x_hi, w_lo)


def _mm(x, w_ref, precise):
    if precise:
        return _dot3(x, w_ref[...])
    return _dot(x.astype(BF16), w_ref[...])


def _layer_norm(x, g, b):
    mu = jnp.mean(x, axis=-1, keepdims=True)
    xc = x - mu
    var = jnp.mean(xc * xc, axis=-1, keepdims=True)
    return xc * lax.rsqrt(var + LN_EPS) * g + b


def _silu(x):
    return x * (1.0 / (1.0 + jnp.exp(-x)))


def _gelu_tanh(x):
    return x * (0.5 * (1.0 + jnp.tanh(0.7978845608028654 * (x + 0.044715 * (x * x * x)))))


def _tile_lanes(t, reps):
    return jnp.concatenate([t] * reps, axis=1)


def _rope(x, cos_t, sin_lo, sin_hi):
    reps = x.shape[1] // LANES
    n = x.shape[1]
    nxt = pltpu.roll(x, n - ROT_DIM // 2, axis=1)
    prv = pltpu.roll(x, ROT_DIM // 2, axis=1)
    return (x * _tile_lanes(cos_t, reps) + nxt * _tile_lanes(sin_lo, reps)
            + prv * _tile_lanes(sin_hi, reps))


def _at(block, *lead):
    nd = len(block)
    return pl.BlockSpec((None,) * len(lead) + tuple(block), lambda *_: tuple(lead) + (0,) * nd)


def _full_spec(shape):
    return _at(shape)


def _ada_kernel(c_ref, w_ref, b_ref, o_ref):
    o_ref[...] = _dot3(_silu(c_ref[...]), w_ref[...]) + b_ref[...]


def _ada_call(c_all, ada_w, ada_b):
    depth, d, d6 = ada_w.shape
    r = c_all.shape[0]
    n_mod = d6 // d
    return pl.pallas_call(
        _ada_kernel,
        out_shape=jax.ShapeDtypeStruct((depth, n_mod, r, d), F32),
        grid=(depth, n_mod),
        in_specs=[
            pl.BlockSpec((r, d), lambda l, j: (0, 0)),
            pl.BlockSpec((None, d, d), lambda l, j: (l, 0, j)),
            pl.BlockSpec((None, None, 1, d), lambda l, j: (l, j, 0, 0)),
        ],
        out_specs=pl.BlockSpec((None, None, r, d), lambda l, j: (l, j, 0, 0)),
        compiler_params=_cparams(2),
        name="ada_mod",
    )(c_all, ada_w, ada_b.reshape(depth, n_mod, 1, d))


class _Group:
    def __init__(self, n_rows, tile, rows_per_cond, cond_row0, mods):
        self.n = n_rows
        self.tile = tile
        self.n_tiles = n_rows // tile
        self.per_row = rows_per_cond == 1
        self.precise = self.per_row
        self.mods = mods if self.per_row else mods.reshape(mods.shape[:3] + (1, mods.shape[3]))
        self.tiles_per_cond = None if self.per_row else rows_per_cond // tile
        self.cond_row0 = cond_row0
        self.d = mods.shape[-1]

    def mod_spec(self, layer, j):
        if self.per_row:
            return pl.BlockSpec((None, None, self.tile, self.d), lambda i, *_: (layer, j, 0, 0))
        tpc, r0 = self.tiles_per_cond, self.cond_row0
        return pl.BlockSpec((None, None, None, 1, self.d),
                            lambda i, *_: (layer, j, r0 + i // tpc, 0, 0))

    def row_spec(self, width):
        return pl.BlockSpec((self.tile, width), lambda i, *_: (i, 0))


def _a_mixer_prompt_kernel(x_ref, sh_ref, sc_ref, gt_ref, win_ref, lng_ref, lnb_ref, ws_ref,
                           bst_ref, wout_ref, ng_ref, nb_ref, o_ref):
    x = x_ref[...]
    t = x.shape[0]
    d_a = lng_ref.shape[1]
    h = (x * (1.0 + sc_ref[...]) + sh_ref[...]).astype(BF16)
    z = _gelu_tanh(_dot(h, win_ref[...]))
    u = z[:, :d_a]
    v = _layer_norm(z[:, d_a:], lng_ref[...], lnb_ref[...]).astype(BF16)
    causal = (lax.broadcasted_iota(jnp.int32, (CHUNK, CHUNK), 0)
              >= lax.broadcasted_iota(jnp.int32, (CHUNK, CHUNK), 1))
    gd = d_a // A_GROUPS
    cols = []
    for g in range(A_GROUPS):
        wsg = jnp.where(causal, ws_ref[g], 0.0).astype(BF16)
        bias = bst_ref[:, g:g + 1]
        rows = [_dot(wsg, v[c * CHUNK:(c + 1) * CHUNK, g * gd:(g + 1) * gd]) + bias
                for c in range(t // CHUNK)]
        cols.append(jnp.concatenate(rows, axis=0))
    s = jnp.concatenate(cols, axis=1)
    y = _dot((u * s).astype(BF16), wout_ref[...])
    o_ref[...] = _layer_norm(DEEPNORM_ALPHA * x + (1.0 + gt_ref[...]) * y, ng_ref[...], nb_ref[...])


def _a_mixer_sample_kernel(x_ref, sh_ref, sc_ref, gt_ref, win_ref, lng_ref, lnb_ref, coef_ref,
                           bias_ref, wout_ref, ng_ref, nb_ref, o_ref, v_ref):
    x = x_ref[...]
    d_a = lng_ref.shape[1]
    h = x * (1.0 + sc_ref[...]) + sh_ref[...]
    z = _gelu_tanh(_dot3(h, win_ref[...]))
    u = z[:, :d_a]
    v = _layer_norm(z[:, d_a:], lng_ref[...], lnb_ref[...])
    v_ref[...] = v
    s = v * coef_ref[...] + bias_ref[...]
    y = _dot3(u * s, wout_ref[...])
    o_ref[...] = _layer_norm(DEEPNORM_ALPHA * x + (1.0 + gt_ref[...]) * y, ng_ref[...], nb_ref[...])


def _a_mixer_call(grp, x, layer, p):
    d = x.shape[1]
    d_a = p["a_ln_g"].shape[-1]
    common_in = [
        grp.row_spec(d), grp.mod_spec(layer, 0), grp.mod_spec(layer, 1), grp.mod_spec(layer, 2),
        _at((d, 2 * d_a), layer), _at((1, d_a), layer), _at((1, d_a), layer),
    ]
    tail_in = [_at((d_a, d), layer), _at((1, d), layer, 0), _at((1, d), layer, 0)]
    args_head = (x, grp.mods, grp.mods, grp.mods, p["a_w_in"], p["a_ln_g"], p["a_ln_b"])
    args_tail = (p["a_w_out"], p["ln_g"], p["ln_b"])
    if grp.per_row:
        return pl.pallas_call(
            _a_mixer_sample_kernel,
            out_shape=(jax.ShapeDtypeStruct((grp.n, d), F32), jax.ShapeDtypeStruct((grp.n, d_a), F32)),
            grid=(grp.n_tiles,),
            in_specs=common_in + [_at((1, d_a), layer), _at((1, d_a), layer)] + tail_in,
            out_specs=(grp.row_spec(d), grp.row_spec(d_a)),
            compiler_params=_cparams(1),
            name="a_mixer_sample",
        )(*args_head, p["a_coef"], p["a_bias"], *args_tail)
    out = pl.pallas_call(
        _a_mixer_prompt_kernel,
        out_shape=jax.ShapeDtypeStruct((grp.n, d), F32),
        grid=(grp.n_tiles,),
        in_specs=common_in + [_at((A_GROUPS, CHUNK, CHUNK), layer), _at((CHUNK, A_GROUPS), layer)] + tail_in,
        out_specs=grp.row_spec(d),
        compiler_params=_cparams(1),
        name="a_mixer_prompt",
    )(*args_head, p["a_w_s"], p["a_b_s_t"], *args_tail)
    return out, None


def _top2_of4(a, b, c, d):
    hi1, lo1 = jnp.maximum(a, b), jnp.minimum(a, b)
    hi2, lo2 = jnp.maximum(c, d), jnp.minimum(c, d)
    return jnp.maximum(hi1, hi2) + jnp.maximum(jnp.minimum(hi1, hi2), jnp.maximum(lo1, lo2))


def _first_argmax4(vals):
    m = jnp.maximum(jnp.maximum(vals[0], vals[1]), jnp.maximum(vals[2], vals[3]))
    idx = jnp.where(vals[0] == m, 0, jnp.where(vals[1] == m, 1, jnp.where(vals[2] == m, 2, 3)))
    return m, idx


def _router_kernel(x_ref, sh_ref, sc_ref, rwt_ref, rb_ref, hr_ref, gid_ref, rank_ref, cnt_ref,
                   carry_ref):
    i = pl.program_id(0)
    t, d = x_ref.shape

    @pl.when(i == 0)
    def _():
        carry_ref[...] = jnp.zeros_like(carry_ref)

    h = x_ref[...] * (1.0 + sc_ref[...]) + sh_ref[...]
    logits = lax.dot_general(rwt_ref[...], h, (((1,), (1,)), ((), ())),
                             precision=lax.Precision.HIGHEST, preferred_element_type=F32)
    mx = jnp.max(logits, axis=0, keepdims=True)
    ex = jnp.exp(logits - mx)
    scores = ex / jnp.sum(ex, axis=0, keepdims=True)
    sel = scores + rb_ref[...]
    sel_r = [sel[e:e + 1, :] for e in range(N_EXPERTS)]
    sc_r = [scores[e:e + 1, :] for e in range(N_EXPERTS)]
    epg = EXPERTS_PER_GROUP
    gs = [_top2_of4(*sel_r[g * epg:(g + 1) * epg]) for g in range(N_EXPERT_GROUPS)]
    best, gid = gs[0], jnp.zeros((1, t), jnp.int32)
    for g in range(1, N_EXPERT_GROUPS):
        upd = gs[g] > best
        gid = jnp.where(upd, g, gid)
        best = jnp.where(upd, gs[g], best)
    cand_sel, cand_sc = [], []
    for k in range(epg):
        a, b = sel_r[k], sc_r[k]
        for g in range(1, N_EXPERT_GROUPS):
            a = jnp.where(gid == g, sel_r[g * epg + k], a)
            b = jnp.where(gid == g, sc_r[g * epg + k], b)
        cand_sel.append(a)
        cand_sc.append(b)
    _, i1 = _first_argmax4(cand_sel)
    rest = [jnp.where(i1 == k, -jnp.inf, cand_sel[k]) for k in range(epg)]
    _, i2 = _first_argmax4(rest)
    w1 = sum(jnp.where(i1 == k, cand_sc[k], 0.0) for k in range(epg))
    w2 = sum(jnp.where(i2 == k, cand_sc[k], 0.0) for k in range(epg))
    tot = w1 + w2
    w1, w2 = w1 / tot, w2 / tot
    gate_rows = [jnp.where(i1 == k, w1, 0.0) + jnp.where(i2 == k, w2, 0.0) for k in range(epg)]
    gates_t = jnp.concatenate(gate_rows + [jnp.zeros((GATE_LANES - epg, t), F32)], axis=0)
    hr_ref[:, :d] = h
    hr_ref[:, d:] = gates_t.T
    onehot = jnp.concatenate(
        [(gid == g).astype(F32) for g in range(N_EXPERT_GROUPS)]
        + [jnp.zeros((SUBLANES - N_EXPERT_GROUPS, t), F32)], axis=0)
    before = (lax.broadcasted_iota(jnp.int32, (t, t), 0)
              < lax.broadcasted_iota(jnp.int32, (t, t), 1)).astype(BF16)
    rank_in_tile = _dot(onehot.astype(BF16), before)
    carry = carry_ref[...]
    rank = jnp.sum(onehot * (rank_in_tile + carry[:, :1]), axis=0, keepdims=True)
    gid_ref[...] = gid
    rank_ref[...] = rank.astype(jnp.int32)
    carry = carry + jnp.sum(onehot, axis=1, keepdims=True)
    carry_ref[...] = carry
    cnt_ref[...] = carry.astype(jnp.int32)


def _router_call(grp, x, layer, p):
    d = x.shape[1]
    per_token = pl.BlockSpec((1, grp.tile), lambda i: (0, i))
    return pl.pallas_call(
        _router_kernel,
        out_shape=(jax.ShapeDtypeStruct((grp.n, d + GATE_LANES), F32),
                   jax.ShapeDtypeStruct((1, grp.n), jnp.int32),
                   jax.ShapeDtypeStruct((1, grp.n), jnp.int32),
                   jax.ShapeDtypeStruct((SUBLANES, LANES), jnp.int32)),
        grid=(grp.n_tiles,),
        in_specs=[grp.row_spec(d), grp.mod_spec(layer, 3), grp.mod_spec(layer, 4),
                  _full_spec((N_EXPERTS, d)), _full_spec((N_EXPERTS, 1))],
        out_specs=(grp.row_spec(d + GATE_LANES), per_token, per_token, _full_spec((SUBLANES, LANES))),
        scratch_shapes=[pltpu.VMEM((SUBLANES, LANES), F32)],
        compiler_params=_cparams(1),
        name="moe_router",
    )(x, grp.mods, grp.mods, p["router_wt"], p["router_b"])


def _row_copy(src_ref, src_row, dst_ref, dst_row, sem):
    return pltpu.make_async_copy(src_ref.at[pl.ds(src_row, 1), :], dst_ref.at[pl.ds(dst_row, 1), :], sem)


def _scatter_kernel(pos_ref, gap_ref, hr_ref, out_ref, zero_ref, sem, zsem, *, tile):
    i = pl.program_id(0)
    base = i * tile

    def issue(c, carry):
        for u in range(DMA_ISSUE_UNROLL):
            r = c * DMA_ISSUE_UNROLL + u
            _row_copy(hr_ref, r, out_ref, pos_ref[base + r], sem).start(priority=u % 2)
        return carry

    lax.fori_loop(0, tile // DMA_ISSUE_UNROLL, issue, 0)

    @pl.when(i == pl.num_programs(0) - 1)
    def _():
        zero_ref[...] = jnp.zeros_like(zero_ref)
        for g in range(N_EXPERT_GROUPS):
            lo, hi = gap_ref[g], gap_ref[N_EXPERT_GROUPS + g]

            def zissue(r, c):
                _row_copy(zero_ref, 0, out_ref, r, zsem).start()
                return c

            def zwait(r, c):
                _row_copy(zero_ref, 0, out_ref, r, zsem).wait()
                return c

            lax.fori_loop(lo, hi, zissue, 0)
            lax.fori_loop(lo, hi, zwait, 0)

    pltpu.make_async_copy(hr_ref, out_ref.at[pl.ds(0, tile), :], sem).wait()


def _scatter_call(grp, hr, pos, gaps, n_rows_out):
    w = hr.shape[1]
    grid_spec = pltpu.PrefetchScalarGridSpec(
        num_scalar_prefetch=2,
        grid=(grp.n_tiles,),
        in_specs=[pl.BlockSpec((grp.tile, w), lambda i, *_: (i, 0))],
        out_specs=pl.BlockSpec(memory_space=pl.ANY),
        scratch_shapes=[pltpu.VMEM((SUBLANES, w), F32), pltpu.SemaphoreType.DMA(()),
                        pltpu.SemaphoreType.DMA(())],
    )
    return pl.pallas_call(
        functools.partial(_scatter_kernel, tile=grp.tile),
        out_shape=jax.ShapeDtypeStruct((n_rows_out, w), F32),
        grid_spec=grid_spec,
        compiler_params=_cparams(1),
        name="moe_scatter",
    )(pos, gaps, hr)


def _experts_kernel(grp_ref, na_ref, rows_ref, wg_ref, wu_ref, wd_ref, o_ref):
    j = pl.program_id(0)

    @pl.when(j >= na_ref[0])
    def _():
        o_ref[...] = jnp.zeros_like(o_ref)

    @pl.when(j < na_ref[0])
    def _():
        d = o_ref.shape[1]
        hb = rows_ref[:, :d].astype(BF16)
        acts = []
        for e in range(EXPERTS_PER_GROUP):
            g = _dot(hb, wg_ref[e])
            u = _dot(hb, wu_ref[e])
            acts.append((_silu(g) * u * rows_ref[:, d + e:d + e + 1]).astype(BF16))
        a = jnp.concatenate(acts, axis=1)
        f = wd_ref.shape[1]
        o_ref[...] = _dot(a, wd_ref[...].reshape(EXPERTS_PER_GROUP * f, d))


def _experts_call(rows, tile_grp, n_active, layer, p, expert_tile):
    n_rows, w = rows.shape
    _, _, epg, d, f = p["moe_w_gate"].shape
    grid_spec = pltpu.PrefetchScalarGridSpec(
        num_scalar_prefetch=2,
        grid=(n_rows // expert_tile,),
        in_specs=[
            pl.BlockSpec((expert_tile, w), lambda j, g, na: (jnp.minimum(j, na[0] - 1), 0)),
            pl.BlockSpec((None, None, epg, d, f), lambda j, g, na: (layer, g[j], 0, 0, 0)),
            pl.BlockSpec((None, None, epg, d, f), lambda j, g, na: (layer, g[j], 0, 0, 0)),
            pl.BlockSpec((None, None, epg, f, d), lambda j, g, na: (layer, g[j], 0, 0, 0)),
        ],
        out_specs=pl.BlockSpec((expert_tile, d), lambda j, g, na: (j, 0)),
    )
    return pl.pallas_call(
        _experts_kernel,
        out_shape=jax.ShapeDtypeStruct((n_rows, d), F32),
        grid_spec=grid_spec,
        compiler_params=_cparams(1),
        name="moe_experts",
    )(tile_grp, n_active, rows, p["moe_w_gate"], p["moe_w_up"], p["moe_w_down"])


def _experts_precise_kernel(grp_ref, na_ref, rows_ref, wg_ref, wu_ref, wd_ref, o_ref):
    j = pl.program_id(0)
    e = pl.program_id(1)

    @pl.when(e == 0)
    def _():
        o_ref[...] = jnp.zeros_like(o_ref)

    @pl.when(j < na_ref[0])
    def _():
        d = o_ref.shape[1]
        h = rows_ref[:, :d]
        gate = sum(jnp.where(e == k, rows_ref[:, d + k:d + k + 1], 0.0) for k in range(EXPERTS_PER_GROUP))
        a = _silu(_dot3(h, wg_ref[...])) * _dot3(h, wu_ref[...]) * gate
        o_ref[...] += _dot3(a, wd_ref[...])


def _experts_precise_call(rows, tile_grp, n_active, layer, p, expert_tile):
    n_rows, w = rows.shape
    _, _, epg, d, f = p["moe_w_gate"].shape

    def w_index(j, k, g, na):
        return (layer, g[j], jnp.where(j < na[0], k, epg - 1), 0, 0)

    grid_spec = pltpu.PrefetchScalarGridSpec(
        num_scalar_prefetch=2,
        grid=(n_rows // expert_tile, epg),
        in_specs=[
            pl.BlockSpec((expert_tile, w), lambda j, k, g, na: (jnp.minimum(j, na[0] - 1), 0)),
            pl.BlockSpec((None, None, None, d, f), w_index),
            pl.BlockSpec((None, None, None, d, f), w_index),
            pl.BlockSpec((None, None, None, f, d), w_index),
        ],
        out_specs=pl.BlockSpec((expert_tile, d), lambda j, k, g, na: (j, 0)),
    )
    return pl.pallas_call(
        _experts_precise_kernel,
        out_shape=jax.ShapeDtypeStruct((n_rows, d), F32),
        grid_spec=grid_spec,
        compiler_params=_cparams(2),
        name="moe_experts_precise",
    )(tile_grp, n_active, rows, p["moe_w_gate"], p["moe_w_up"], p["moe_w_down"])


def _row_plan(counts, gid, rank, expert_tile, n_tiles):
    nt = (counts + expert_tile - 1) // expert_tile
    cum = jnp.cumsum(nt)
    starts = (cum - nt) * expert_tile
    pos = starts[gid] + rank
    gap_lo = starts + counts
    gap_hi = jnp.concatenate([starts[1:], jnp.full((1,), n_tiles * expert_tile, jnp.int32)])
    n_active = cum[-1]
    j = jnp.minimum(jnp.arange(n_tiles, dtype=jnp.int32), n_active - 1)
    tile_grp = jnp.sum((j[:, None] >= cum[None, :]).astype(jnp.int32), axis=1)
    gaps = jnp.concatenate([gap_lo, gap_hi]).astype(jnp.int32)
    return pos.astype(jnp.int32), gaps, tile_grp.astype(jnp.int32), n_active.reshape(1).astype(jnp.int32)


def _moe_post_kernel(pos_ref, x_ref, gt_ref, ng_ref, nb_ref, f_ref, o_ref, buf_ref, sem, *, tile):
    i = pl.program_id(0)
    n = pl.num_programs(0)

    def issue(step, slot):
        base = step * tile

        def body(c, carry):
            for u in range(DMA_ISSUE_UNROLL):
                r = c * DMA_ISSUE_UNROLL + u
                _row_copy(f_ref, pos_ref[base + r], buf_ref.at[slot], r, sem.at[slot]).start(priority=u % 2)
            return carry

        lax.fori_loop(0, tile // DMA_ISSUE_UNROLL, body, 0)

    @pl.when(i == 0)
    def _():
        issue(0, 0)

    slot = i % 2

    @pl.when(i + 1 < n)
    def _():
        issue(i + 1, 1 - slot)

    pltpu.make_async_copy(f_ref.at[pl.ds(0, tile), :], buf_ref.at[slot], sem.at[slot]).wait()
    x = x_ref[...]
    o_ref[...] = _layer_norm(DEEPNORM_ALPHA * x + (1.0 + gt_ref[...]) * buf_ref[slot],
                             ng_ref[...], nb_ref[...])


def _moe_post_call(grp, x, layer, f_rows, pos, p):
    d = x.shape[1]
    grid_spec = pltpu.PrefetchScalarGridSpec(
        num_scalar_prefetch=1,
        grid=(grp.n_tiles,),
        in_specs=[grp.row_spec(d), grp.mod_spec(layer, 5), _at((1, d), layer, 1), _at((1, d), layer, 1),
                  pl.BlockSpec(memory_space=pl.ANY)],
        out_specs=grp.row_spec(d),
        scratch_shapes=[pltpu.VMEM((2, grp.tile, d), F32), pltpu.SemaphoreType.DMA((2,))],
    )
    return pl.pallas_call(
        functools.partial(_moe_post_kernel, tile=grp.tile),
        out_shape=jax.ShapeDtypeStruct((grp.n, d), F32),
        grid_spec=grid_spec,
        compiler_params=_cparams(1),
        name="moe_post",
    )(pos, x, grp.mods, p["ln_g"], p["ln_b"], f_rows)


def _moe_sublayer(grp, x, layer, p, expert_tile):
    hr, gid, rank, cnt = _router_call(grp, x, layer, p)
    n_tiles = grp.n // expert_tile + N_EXPERT_GROUPS
    pos, gaps, tile_grp, n_active = _row_plan(cnt[:N_EXPERT_GROUPS, 0], gid.reshape(grp.n),
                                              rank.reshape(grp.n), expert_tile, n_tiles)
    rows = _scatter_call(grp, hr, pos, gaps, n_tiles * expert_tile)
    experts = _experts_precise_call if grp.precise else _experts_call
    f_rows = experts(rows, tile_grp, n_active, layer, p, expert_tile)
    return _moe_post_call(grp, x, layer, f_rows, pos, p)


def _kv_kernel(x_ref, wk_ref, wv_ref, cos_ref, slo_ref, shi_ref, *out_refs, precise):
    x = x_ref[...]
    k = _rope(_mm(x, wk_ref, precise), cos_ref[...], slo_ref[...], shi_ref[...])
    v = _mm(x, wv_ref, precise)
    if len(out_refs) == 2:
        out_refs[0][...] = k
        out_refs[1][...] = v
    else:
        kt = k.T
        out_refs[0][...] = kt
        out_refs[1][...] = v
        out_refs[2][...] = kt.astype(BF16)
        out_refs[3][...] = v.astype(BF16)


def _rope_spec(grp, tables_rows):
    if tables_rows == 1:
        return _full_spec((1, LANES))
    tiles = tables_rows // grp.tile
    return pl.BlockSpec((grp.tile, LANES), lambda i, *_: (i % tiles, 0))


def _kv_call(grp, x, p, rope_tabs, seq):
    d = x.shape[1]
    wk_cols, wv_cols = p["kv_w_k"].shape[1], p["kv_w_v"].shape[1]
    rs = _rope_spec(grp, rope_tabs[0].shape[0])
    if seq is None:
        out_shape = [jax.ShapeDtypeStruct((grp.n, wk_cols), F32), jax.ShapeDtypeStruct((grp.n, wv_cols), F32)]
        out_specs = [grp.row_spec(wk_cols), grp.row_spec(wv_cols)]
    else:
        tiles = seq // grp.tile
        kt_spec = pl.BlockSpec((None, wk_cols, grp.tile), lambda i: (i // tiles, 0, i % tiles))
        out_shape = [jax.ShapeDtypeStruct((grp.n // seq, wk_cols, seq), F32),
                     jax.ShapeDtypeStruct((grp.n, wv_cols), F32),
                     jax.ShapeDtypeStruct((grp.n // seq, wk_cols, seq), BF16),
                     jax.ShapeDtypeStruct((grp.n, wv_cols), BF16)]
        out_specs = [kt_spec, grp.row_spec(wv_cols), kt_spec, grp.row_spec(wv_cols)]
    return pl.pallas_call(
        functools.partial(_kv_kernel, precise=grp.precise),
        out_shape=tuple(out_shape),
        grid=(grp.n_tiles,),
        in_specs=[grp.row_spec(d), _full_spec((d, wk_cols)), _full_spec((d, wv_cols)), rs, rs, rs],
        out_specs=tuple(out_specs),
        compiler_params=_cparams(1),
        name="kv_proj",
    )(x, p["kv_w_k"], p["kv_w_v"], *rope_tabs)


def _q_kernel(x_ref, sh_ref, sc_ref, wq_ref, cos_ref, slo_ref, shi_ref, o_ref, *, precise):
    h = x_ref[...] * (1.0 + sc_ref[...]) + sh_ref[...]
    q = _rope(_mm(h, wq_ref, precise), cos_ref[...], slo_ref[...], shi_ref[...])
    o_ref[...] = (q * (HEAD_DIM ** -0.5)).astype(o_ref.dtype)


def _q_call(grp, x, layer, p, rope_tabs, out_dtype):
    d = x.shape[1]
    j = layer - N_A_LAYERS
    cols = p["b_w_q"].shape[-1]
    rs = _rope_spec(grp, rope_tabs[0].shape[0])
    return pl.pallas_call(
        functools.partial(_q_kernel, precise=grp.precise),
        out_shape=jax.ShapeDtypeStruct((grp.n, cols), out_dtype),
        grid=(grp.n_tiles,),
        in_specs=[grp.row_spec(d), grp.mod_spec(layer, 0), grp.mod_spec(layer, 1),
                  _at((d, cols), j), rs, rs, rs],
        out_specs=grp.row_spec(cols),
        compiler_params=_cparams(1),
        name="q_proj",
    )(x, grp.mods, grp.mods, p["b_w_q"], *rope_tabs)


def _diff_lambda(lamv_ref, lam_init):
    lv = lamv_ref[...]
    a = jnp.exp(jnp.sum(lv[0:1] * lv[1:2], axis=1, keepdims=True))
    b = jnp.exp(jnp.sum(lv[2:3] * lv[3:4], axis=1, keepdims=True))
    return a - b + lam_init


def _attn_prompt_kernel(q_ref, kt_ref, v_ref, lamv_ref, sg_ref, o_ref, s_ref, p_ref, v1_ref,
                        *, lam_init, t):
    seq = q_ref.shape[0]
    rows_per_chunk = SOFTMAX_ROWS
    v1_ref[:, :V_DIM] = v_ref[...]
    v1_ref[:, V_DIM:] = jnp.ones((seq, V_DIM), BF16)
    lam = _diff_lambda(lamv_ref, lam_init)
    out_scale = sg_ref[...] * (1.0 - lam_init)
    lane = lax.broadcasted_iota(jnp.int32, (t, V_DIM), 1)
    row_i = lax.broadcasted_iota(jnp.int32, (rows_per_chunk, LANES), 0)
    col_i = lax.broadcasted_iota(jnp.int32, (rows_per_chunk, LANES), 1)
    for qi in range(seq // t):
        slot = qi % 2
        kend = (qi + 1) * t
        q = q_ref[qi * t:(qi + 1) * t, :]
        zero = jnp.zeros_like(q)
        qq = jnp.concatenate([jnp.where(lane < HEAD_DIM, q, zero), jnp.where(lane >= HEAD_DIM, q, zero)], axis=0)
        s_ref[slot, :, :kend] = _dot(qq, kt_ref[:, :kend])
        for rc in range(2 * t // rows_per_chunk):
            r0 = rc * rows_per_chunk
            q0 = r0 % t
            rows = slice(r0, r0 + rows_per_chunk)

            def load(cb):
                sc = s_ref[slot, rows, cb * LANES:(cb + 1) * LANES]
                c0 = cb * LANES - qi * t
                if c0 + LANES - 1 > q0:
                    sc = jnp.where(col_i + c0 <= row_i + q0, sc, NEG_BIG)
                return sc

            n_cb = min(kend, qi * t + q0 + rows_per_chunk + LANES - 1) // LANES
            n_cb = min(n_cb, kend // LANES)
            m = load(0)
            for cb in range(1, n_cb):
                m = jnp.maximum(m, load(cb))
            m = jnp.broadcast_to(jnp.max(m, axis=1, keepdims=True), m.shape)
            for cb in range(kend // LANES):
                cols = slice(cb * LANES, (cb + 1) * LANES)
                if cb < n_cb:
                    p_ref[slot, rows, cols] = jnp.exp(load(cb) - m).astype(BF16)
                else:
                    p_ref[slot, rows, cols] = jnp.zeros((rows_per_chunk, LANES), BF16)
        acc = _dot(p_ref[slot, :, :kend], v1_ref[:kend, :])
        o = acc[:, :V_DIM] / acc[:, V_DIM:]
        od = o[:t] - lam * o[t:]
        ms = jnp.mean(od * od, axis=-1, keepdims=True)
        o_ref[qi * t:(qi + 1) * t, :] = (od * lax.rsqrt(ms + LN_EPS) * out_scale).astype(o_ref.dtype)


def _attn_prompt_call(q, kt, v, layer, p, lam_init, batch, seq, t):
    n, cols = q.shape
    j = layer - N_A_LAYERS
    per_head = pl.BlockSpec((seq, V_DIM), lambda b, h: (b, h))
    return pl.pallas_call(
        functools.partial(_attn_prompt_kernel, lam_init=lam_init, t=t),
        out_shape=jax.ShapeDtypeStruct((n, cols), BF16),
        grid=(batch, N_HEADS),
        in_specs=[per_head, pl.BlockSpec((None, V_DIM, seq), lambda b, h: (b, h, 0)), per_head,
                  _at((4, HEAD_DIM), j), _at((1, V_DIM), j)],
        out_specs=per_head,
        scratch_shapes=[pltpu.VMEM((2, 2 * t, seq), F32), pltpu.VMEM((2, 2 * t, seq), BF16),
                        pltpu.VMEM((seq, 2 * V_DIM), BF16)],
        compiler_params=_cparams(2),
        name="attn_prompt",
    )(q, kt, v, p["lamv"], p["subln_g"])


def _attn_sample_kernel(pt_ref, q_ref, kn_ref, vn_ref, lamv_ref, sg_ref, *rest, lam_init):
    del pt_ref
    npg = PAGES_PER_STEP
    k_refs, v_refs = rest[:npg], rest[npg:2 * npg]
    o_ref, qcol_ref, m_ref, l_ref, acc_ref = rest[2 * npg:]
    step = pl.program_id(1)
    n_rows = 2 * N_HEADS
    cols = q_ref.shape[1]

    def seg_sums(prod):
        return jnp.sum(prod.reshape(n_rows, HEAD_DIM, prod.shape[1]), axis=1)

    def as_column(row_vec):
        return jnp.transpose(jnp.broadcast_to(row_vec, (LANES, cols)))

    @pl.when(step == 0)
    def _():
        qcol = as_column(q_ref[...])
        qcol_ref[...] = qcol
        m_ref[...] = seg_sums(qcol * as_column(kn_ref[...]))[:, :1]
        l_ref[...] = jnp.ones_like(l_ref)
        vn = vn_ref[...]
        acc_ref[...] = jnp.concatenate(
            [jnp.broadcast_to(vn[:, h * V_DIM:(h + 1) * V_DIM], (2, V_DIM)) for h in range(N_HEADS)], axis=0)

    qcol = qcol_ref[...]
    s = jnp.concatenate([seg_sums(r[...] * qcol) for r in k_refs], axis=1)
    m_old = m_ref[...]
    m_new = jnp.maximum(m_old, jnp.max(s, axis=1, keepdims=True))
    a = jnp.exp(m_old - m_new)
    p = jnp.exp(s - m_new)
    l_ref[...] = a * l_ref[...] + jnp.sum(p, axis=1, keepdims=True)
    m_ref[...] = m_new
    p_hi, p_lo = _split_bf16(p)
    prow = lax.broadcasted_iota(jnp.int32, p.shape, 0)
    zero = jnp.zeros((), BF16)
    pv = jnp.zeros(acc_ref.shape, F32)
    for h in range(N_HEADS):
        vh = jnp.concatenate([r[pl.ds(h, PAGE_SIZE, stride=N_HEADS), :] for r in v_refs], axis=0)
        v_hi, v_lo = _split_bf16(vh)
        mine = (prow // 2) == h
        ph_hi = jnp.where(mine, p_hi, zero)
        r = _dot(jnp.concatenate([ph_hi, jnp.where(mine, p_lo, zero)], axis=0), v_hi)
        pv = pv + r[:n_rows] + r[n_rows:] + _dot(ph_hi, v_lo)
    acc_ref[...] = a * acc_ref[...] + pv

    @pl.when(step == pl.num_programs(1) - 1)
    def _():
        o = acc_ref[...] / l_ref[...]
        lam = _diff_lambda(lamv_ref, lam_init)
        outs = []
        for h in range(N_HEADS):
            od = o[2 * h:2 * h + 1] - lam * o[2 * h + 1:2 * h + 2]
            ms = jnp.mean(od * od, axis=-1, keepdims=True)
            outs.append(od * lax.rsqrt(ms + LN_EPS) * sg_ref[...] * (1.0 - lam_init))
        o_ref[...] = jnp.concatenate(outs, axis=1)


def _attn_sample_call(q, k_new, v_new, cache_kt, cache_v2, page_table, layer, p, lam_init):
    n_seq, cols = q.shape
    n_pages = page_table.shape[1]
    npg = PAGES_PER_STEP
    j = layer - N_A_LAYERS
    row3 = lambda a: a.reshape(n_seq, 1, cols)
    row_spec = pl.BlockSpec((None, 1, cols), lambda b, g, pt: (b, 0, 0))

    def page_spec(i, shape):
        return pl.BlockSpec((None,) + shape, lambda b, g, pt: (pt[b * n_pages + g * npg + i], 0, 0))

    grid_spec = pltpu.PrefetchScalarGridSpec(
        num_scalar_prefetch=1,
        grid=(n_seq, n_pages // npg),
        in_specs=[row_spec, row_spec, row_spec, _at((4, HEAD_DIM), j), _at((1, V_DIM), j)]
        + [page_spec(i, cache_kt.shape[1:]) for i in range(npg)]
        + [page_spec(i, cache_v2.shape[1:]) for i in range(npg)],
        out_specs=row_spec,
        scratch_shapes=[pltpu.VMEM((cols, LANES), F32), pltpu.VMEM((2 * N_HEADS, 1), F32),
                        pltpu.VMEM((2 * N_HEADS, 1), F32), pltpu.VMEM((2 * N_HEADS, V_DIM), F32)],
    )
    out = pl.pallas_call(
        functools.partial(_attn_sample_kernel, lam_init=lam_init),
        out_shape=jax.ShapeDtypeStruct((n_seq, 1, cols), F32),
        grid_spec=grid_spec,
        compiler_params=_cparams(2),
        name="attn_sample",
    )(page_table.reshape(-1), row3(q), row3(k_new), row3(v_new), p["lamv"], p["subln_g"],
      *([cache_kt] * npg), *([cache_v2] * npg))
    return out.reshape(n_seq, cols)


def _attn_out_kernel(o_ref, x_ref, gt_ref, wo_ref, ng_ref, nb_ref, y_ref, *, precise):
    y = _mm(o_ref[...], wo_ref, precise)
    x = x_ref[...]
    y_ref[...] = _layer_norm(DEEPNORM_ALPHA * x + (1.0 + gt_ref[...]) * y, ng_ref[...], nb_ref[...])


def _attn_out_call(grp, o, x, layer, p):
    d = x.shape[1]
    cols = o.shape[1]
    j = layer - N_A_LAYERS
    return pl.pallas_call(
        functools.partial(_attn_out_kernel, precise=grp.precise),
        out_shape=jax.ShapeDtypeStruct((grp.n, d), F32),
        grid=(grp.n_tiles,),
        in_specs=[grp.row_spec(cols), grp.row_spec(d), grp.mod_spec(layer, 2), _at((cols, d), j),
                  _at((1, d), layer, 0), _at((1, d), layer, 0)],
        out_specs=grp.row_spec(d),
        compiler_params=_cparams(1),
        name="attn_out",
    )(o, x, grp.mods, p["b_w_o"], p["ln_g"], p["ln_b"])


def _rope_tables(pos):
    half = ROT_DIM // 2
    inv = 1.0 / (ROPE_THETA ** (jnp.arange(0, ROT_DIM, 2, dtype=F32) / ROT_DIM))
    ang = pos.astype(F32)[:, None] * inv[None, :]
    cos, sin = jnp.cos(ang), jnp.sin(ang)
    n = pos.shape[0]
    ones = jnp.ones((n, HEAD_DIM - ROT_DIM), F32)
    zeros = jnp.zeros((n, HEAD_DIM - ROT_DIM), F32)
    zh = jnp.zeros((n, half), F32)
    cos_t = jnp.concatenate([cos, cos, ones], axis=1)
    sin_lo = jnp.concatenate([-sin, zh, zeros], axis=1)
    sin_hi = jnp.concatenate([zh, sin, zeros], axis=1)
    rep = LANES // HEAD_DIM
    return tuple(jnp.tile(t, (1, rep)) for t in (cos_t, sin_lo, sin_hi))


def _trunk(grp, x, p, rope_tabs, attend, expert_tile, seq):
    v_rows = []
    k_sh = v_sh = kv_attn = None
    for l in range(DEPTH):
        if l < N_A_LAYERS:
            x, v_a = _a_mixer_call(grp, x, l, p)
            v_rows.append(v_a)
        else:
            lam_init = 0.8 - 0.6 * math.exp(-0.3 * l)
            q = _q_call(grp, x, l, p, rope_tabs, F32 if seq is None else BF16)
            o = attend(q, kv_attn, l, p, lam_init)
            x = _attn_out_call(grp, o, x, l, p)
        x = _moe_sublayer(grp, x, l, p, expert_tile)
        if l == N_A_LAYERS - 1:
            outs = _kv_call(grp, x, p, rope_tabs, seq)
            k_sh, v_sh = outs[0], outs[1]
            kv_attn = (k_sh, v_sh) if seq is None else (outs[2], outs[3])
    return x, k_sh, v_sh, v_rows


def kernel(x_prompt, x_sample, cache_k, cache_v, page_table, c_prompt, c_sample, ada_w, ada_b, ln_g, ln_b, a_w_in, a_ln_g, a_ln_b, a_w_s, a_b_s, a_w_out, kv_w_k, kv_w_v, b_w_q, b_lam_q1, b_lam_k1, b_lam_q2, b_lam_k2, b_subln_g, b_w_o, router_w, router_bias, moe_w_gate, moe_w_up, moe_w_down):
    batch, seq, d = x_prompt.shape
    dec_batch, dec_seq, _ = x_sample.shape
    assert dec_seq == 1, "the sample group is one new token per sequence"
    n_pool = cache_k.shape[0]
    past = page_table.shape[1] * PAGE_SIZE
    depth, n_exp, _, d_exp = moe_w_gate.shape
    n_a, d_a = a_ln_g.shape
    n_b = b_w_q.shape[0]
    gd = d_a // A_GROUPS
    epg = EXPERTS_PER_GROUP

    p_f32 = dict(
        a_w_in=a_w_in, a_w_out=a_w_out, kv_w_k=kv_w_k, kv_w_v=kv_w_v, b_w_q=b_w_q, b_w_o=b_w_o,
        moe_w_gate=moe_w_gate.reshape(depth, n_exp // epg, epg, d, d_exp),
        moe_w_up=moe_w_up.reshape(depth, n_exp // epg, epg, d, d_exp),
        moe_w_down=moe_w_down.reshape(depth, n_exp // epg, epg, d_exp, d),
        a_ln_g=a_ln_g.reshape(n_a, 1, d_a), a_ln_b=a_ln_b.reshape(n_a, 1, d_a),
        a_w_s=a_w_s, a_b_s_t=jnp.swapaxes(a_b_s, 1, 2),
        a_coef=jnp.repeat(a_w_s[:, :, 0, 0], gd, axis=1).reshape(n_a, 1, d_a),
        a_bias=jnp.repeat(a_b_s[:, :, 0], gd, axis=1).reshape(n_a, 1, d_a),
        ln_g=ln_g.reshape(depth, 2, 1, d), ln_b=ln_b.reshape(depth, 2, 1, d),
        lamv=jnp.stack([b_lam_q1, b_lam_k1, b_lam_q2, b_lam_k2], axis=1),
        subln_g=b_subln_g.reshape(n_b, 1, V_DIM),
        router_wt=router_w.T, router_b=router_bias.reshape(N_EXPERTS, 1),
    )
    matmul_weights = ("a_w_in", "a_w_out", "kv_w_k", "kv_w_v", "b_w_q", "b_w_o",
                      "moe_w_gate", "moe_w_up", "moe_w_down")
    p_bf16 = {k: (v.astype(BF16) if k in matmul_weights else v) for k, v in p_f32.items()}

    mods = _ada_call(jnp.concatenate([c_sample, c_prompt], axis=0), ada_w, ada_b)

    gp = _Group(batch * seq, PROMPT_TILE, seq, dec_batch, mods)
    tabs_p = _rope_tables(jnp.arange(seq, dtype=jnp.int32))

    def attend_prompt(q, kv, layer, p, lam_init):
        return _attn_prompt_call(q, kv[0], kv[1], layer, p, lam_init, batch, seq, ATTN_TILE)

    y_p, kt_p, v_p, _ = _trunk(gp, x_prompt.reshape(batch * seq, d), p_bf16, tabs_p, attend_prompt,
                               PROMPT_EXPERT_TILE, seq)
    k_p = jnp.transpose(kt_p.reshape(batch, N_HEADS, 2, HEAD_DIM, seq), (0, 4, 1, 2, 3))

    gs = _Group(dec_batch, dec_batch, 1, 0, mods)
    tabs_s = _rope_tables(jnp.full((1,), past, dtype=jnp.int32))
    cache_kt = jnp.transpose(cache_k, (0, 2, 3, 4, 1)).reshape(n_pool, N_HEADS * 2 * HEAD_DIM, PAGE_SIZE)
    cache_v2 = cache_v.reshape(n_pool, PAGE_SIZE * N_HEADS, V_DIM)

    def attend_sample(q, kv, layer, p, lam_init):
        return _attn_sample_call(q, kv[0], kv[1], cache_kt, cache_v2, page_table, layer, p, lam_init)

    y_s, k_s, v_s, v_rows = _trunk(gs, x_sample.reshape(dec_batch, d), p_f32, tabs_s, attend_sample,
                                   dec_batch, None)

    return (
        y_p.reshape(batch, seq, d),
        y_s.reshape(dec_batch, 1, d),
        k_p,
        v_p.reshape(batch, seq, N_HEADS, V_DIM),
        k_s.reshape(dec_batch, 1, N_HEADS, 2, HEAD_DIM),
        v_s.reshape(dec_batch, 1, N_HEADS, V_DIM),
        jnp.stack(v_rows, axis=0).reshape(N_A_LAYERS, dec_batch, 1, d_a),
    )
```

```python
import functools
import math

import jax
import jax.numpy as jnp
from jax import lax
from jax.experimental import pallas as pl
from jax.experimental.pallas import tpu as pltpu

DEPTH = 4
N_A_LAYERS = DEPTH // 2
CHUNK = 128
A_GROUPS = 8
N_HEADS = 8
HEAD_DIM = 64
V_DIM = 2 * HEAD_DIM
ROT_DIM = HEAD_DIM // 4
ROPE_THETA = 500000.0
N_EXPERTS = 16
N_EXPERT_GROUPS = 4
EXPERTS_PER_GROUP = N_EXPERTS // N_EXPERT_GROUPS
LN_EPS = 1e-5
DEEPNORM_ALPHA = (2 * DEPTH) ** 0.25
PAGE_SIZE = 128

LANES = 128
SUBLANES = 8
BF16_SUBLANES = 16
VMEM_LIMIT_BYTES = 56 * 1024 * 1024

GATE_LANES = LANES
NEG_BIG = -1e30
PAGES_PER_STEP = 16
SOFTMAX_ROWS = 32

PROMPT_TILE = 512
PROMPT_EXPERT_TILE = 512
ATTN_TILE = 256

F32 = jnp.float32
BF16 = jnp.bfloat16


def _cparams(n_axes):
    return pltpu.CompilerParams(
        dimension_semantics=("arbitrary",) * n_axes, vmem_limit_bytes=VMEM_LIMIT_BYTES)


def _dot(a, b):
    return jnp.dot(a, b, preferred_element_type=F32)


def _dot_nt(a, b):
    return lax.dot_general(a, b, (((1,), (1,)), ((), ())), preferred_element_type=F32)


def _split_bf16(a):
    hi = a.astype(BF16)
    return hi, (a - hi.astype(F32)).astype(BF16)


def _dot3(x, w):
    x_hi, x_lo = _split_bf16(x)
    w_hi, w_lo = _split_bf16(w)
    m = x.shape[0]
    if m % BF16_SUBLANES == 0:
        r = _dot(jnp.concatenate([x_hi, x_lo], axis=0), w_hi)
        return r[:m] + r[m:] + _dot(x_hi, w_lo)
    return _dot(x_hi, w_hi) + _dot(x_lo, w_hi) + _dot(x_hi, w_lo)


def _mm(x, w_ref, precise):
    if precise:
        return _dot3(x, w_ref[...])
    return _dot(x.astype(BF16), w_ref[...])


def _layer_norm(x, g, b):
    mu = jnp.mean(x, axis=-1, keepdims=True)
    xc = x - mu
    var = jnp.mean(xc * xc, axis=-1, keepdims=True)
    return xc * lax.rsqrt(var + LN_EPS) * g + b


def _silu(x):
    return x * (1.0 / (1.0 + jnp.exp(-x)))


def _gelu_tanh(x):
    return x * (0.5 * (1.0 + jnp.tanh(0.7978845608028654 * (x + 0.044715 * (x * x * x)))))


def _tile_lanes(t, reps):
    return jnp.concatenate([t] * reps, axis=1)


def _rope(x, cos_t, sin_lo, sin_hi):
    reps = x.shape[1] // LANES
    n = x.shape[1]
    nxt = pltpu.roll(x, n - ROT_DIM // 2, axis=1)
    prv = pltpu.roll(x, ROT_DIM // 2, axis=1)
    return (x * _tile_lanes(cos_t, reps) + nxt * _tile_lanes(sin_lo, reps)
            + prv * _tile_lanes(sin_hi, reps))


def _at(block, *lead):
    nd = len(block)
    return pl.BlockSpec((None,) * len(lead) + tuple(block), lambda *_: tuple(lead) + (0,) * nd)


def _full_spec(shape):
    return _at(shape)


def _ada_kernel(c_ref, w_ref, b_ref, o_ref):
    o_ref[...] = _dot3(_silu(c_ref[...]), w_ref[...]) + b_ref[...]


def _ada_call(c_all, ada_w, ada_b):
    depth, d, d6 = ada_w.shape
    r = c_all.shape[0]
    n_mod = d6 // d
    return pl.pallas_call(
        _ada_kernel,
        out_shape=jax.ShapeDtypeStruct((depth, n_mod, r, d), F32),
        grid=(depth, n_mod),
        in_specs=[
            pl.BlockSpec((r, d), lambda l, j: (0, 0)),
            pl.BlockSpec((None, d, d), lambda l, j: (l, 0, j)),
            pl.BlockSpec((None, None, 1, d), lambda l, j: (l, j, 0, 0)),
        ],
        out_specs=pl.BlockSpec((None, None, r, d), lambda l, j: (l, j, 0, 0)),
        compiler_params=_cparams(2),
        name="ada_mod",
    )(c_all, ada_w, ada_b.reshape(depth, n_mod, 1, d))


class _Group:
    def __init__(self, n_rows, tile, rows_per_cond, cond_row0, mods):
        self.n = n_rows
        self.tile = tile
        self.n_tiles = n_rows // tile
        self.per_row = rows_per_cond == 1
        self.precise = self.per_row
        self.mods = mods if self.per_row else mods.reshape(mods.shape[:3] + (1, mods.shape[3]))
        self.tiles_per_cond = None if self.per_row else rows_per_cond // tile
        self.cond_row0 = cond_row0
        self.d = mods.shape[-1]

    def mod_spec(self, layer, j):
        if self.per_row:
            return pl.BlockSpec((None, None, self.tile, self.d), lambda i, *_: (layer, j, 0, 0))
        tpc, r0 = self.tiles_per_cond, self.cond_row0
        return pl.BlockSpec((None, None, None, 1, self.d),
                            lambda i, *_: (layer, j, r0 + i // tpc, 0, 0))

    def row_spec(self, width):
        return pl.BlockSpec((self.tile, width), lambda i, *_: (i, 0))


def _a_mixer_prompt_kernel(x_ref, sh_ref, sc_ref, gt_ref, win_ref, lng_ref, lnb_ref, ws_ref,
                           bst_ref, wout_ref, ng_ref, nb_ref, o_ref):
    x = x_ref[...]
    t = x.shape[0]
    d_a = lng_ref.shape[1]
    h = (x * (1.0 + sc_ref[...]) + sh_ref[...]).astype(BF16)
    z = _gelu_tanh(_dot(h, win_ref[...]))
    u = z[:, :d_a]
    v = _layer_norm(z[:, d_a:], lng_ref[...], lnb_ref[...]).astype(BF16)
    causal = (lax.broadcasted_iota(jnp.int32, (CHUNK, CHUNK), 0)
              >= lax.broadcasted_iota(jnp.int32, (CHUNK, CHUNK), 1))
    gd = d_a // A_GROUPS
    cols = []
    for g in range(A_GROUPS):
        wsg = jnp.where(causal, ws_ref[g], 0.0).astype(BF16)
        bias = bst_ref[:, g:g + 1]
        rows = [_dot(wsg, v[c * CHUNK:(c + 1) * CHUNK, g * gd:(g + 1) * gd]) + bias
                for c in range(t // CHUNK)]
        cols.append(jnp.concatenate(rows, axis=0))
    s = jnp.concatenate(cols, axis=1)
    y = _dot((u * s).astype(BF16), wout_ref[...])
    o_ref[...] = _layer_norm(DEEPNORM_ALPHA * x + (1.0 + gt_ref[...]) * y, ng_ref[...], nb_ref[...])


def _a_mixer_sample_kernel(x_ref, sh_ref, sc_ref, gt_ref, win_ref, lng_ref, lnb_ref, coef_ref,
                           bias_ref, wout_ref, ng_ref, nb_ref, o_ref, v_ref):
    x = x_ref[...]
    d_a = lng_ref.shape[1]
    h = x * (1.0 + sc_ref[...]) + sh_ref[...]
    z = _gelu_tanh(_dot3(h, win_ref[...]))
    u = z[:, :d_a]
    v = _layer_norm(z[:, d_a:], lng_ref[...], lnb_ref[...])
    v_ref[...] = v
    s = v * coef_ref[...] + bias_ref[...]
    y = _dot3(u * s, wout_ref[...])
    o_ref[...] = _layer_norm(DEEPNORM_ALPHA * x + (1.0 + gt_ref[...]) * y, ng_ref[...], nb_ref[...])


def _a_mixer_call(grp, x, layer, p):
    d = x.shape[1]
    d_a = p["a_ln_g"].shape[-1]
    common_in = [
        grp.row_spec(d), grp.mod_spec(layer, 0), grp.mod_spec(layer, 1), grp.mod_spec(layer, 2),
        _at((d, 2 * d_a), layer), _at((1, d_a), layer), _at((1, d_a), layer),
    ]
    tail_in = [_at((d_a, d), layer), _at((1, d), layer, 0), _at((1, d), layer, 0)]
    args_head = (x, grp.mods, grp.mods, grp.mods, p["a_w_in"], p["a_ln_g"], p["a_ln_b"])
    args_tail = (p["a_w_out"], p["ln_g"], p["ln_b"])
    if grp.per_row:
        return pl.pallas_call(
            _a_mixer_sample_kernel,
            out_shape=(jax.ShapeDtypeStruct((grp.n, d), F32), jax.ShapeDtypeStruct((grp.n, d_a), F32)),
            grid=(grp.n_tiles,),
            in_specs=common_in + [_at((1, d_a), layer), _at((1, d_a), layer)] + tail_in,
            out_specs=(grp.row_spec(d), grp.row_spec(d_a)),
            compiler_params=_cparams(1),
            name="a_mixer_sample",
        )(*args_head, p["a_coef"], p["a_bias"], *args_tail)
    out = pl.pallas_call(
        _a_mixer_prompt_kernel,
        out_shape=jax.ShapeDtypeStruct((grp.n, d), F32),
        grid=(grp.n_tiles,),
        in_specs=common_in + [_at((A_GROUPS, CHUNK, CHUNK), layer), _at((CHUNK, A_GROUPS), layer)] + tail_in,
        out_specs=grp.row_spec(d),
        compiler_params=_cparams(1),
        name="a_mixer_prompt",
    )(*args_head, p["a_w_s"], p["a_b_s_t"], *args_tail)
    return out, None


def _top2_of4(a, b, c, d):
    hi1, lo1 = jnp.maximum(a, b), jnp.minimum(a, b)
    hi2, lo2 = jnp.maximum(c, d), jnp.minimum(c, d)
    return jnp.maximum(hi1, hi2) + jnp.maximum(jnp.minimum(hi1, hi2), jnp.maximum(lo1, lo2))


def _first_argmax4(vals):
    m = jnp.maximum(jnp.maximum(vals[0], vals[1]), jnp.maximum(vals[2], vals[3]))
    idx = jnp.where(vals[0] == m, 0, jnp.where(vals[1] == m, 1, jnp.where(vals[2] == m, 2, 3)))
    return m, idx


def _router_kernel(x_ref, sh_ref, sc_ref, rwt_ref, rb_ref, hr_ref, gid_ref, rank_ref, cnt_ref,
                   carry_ref):
    i = pl.program_id(0)
    t, d = x_ref.shape

    @pl.when(i == 0)
    def _():
        carry_ref[...] = jnp.zeros_like(carry_ref)

    h = x_ref[...] * (1.0 + sc_ref[...]) + sh_ref[...]
    logits = lax.dot_general(rwt_ref[...], h, (((1,), (1,)), ((), ())),
                             precision=lax.Precision.HIGHEST, preferred_element_type=F32)
    mx = jnp.max(logits, axis=0, keepdims=True)
    ex = jnp.exp(logits - mx)
    scores = ex / jnp.sum(ex, axis=0, keepdims=True)
    sel = scores + rb_ref[...]
    sel_r = [sel[e:e + 1, :] for e in range(N_EXPERTS)]
    sc_r = [scores[e:e + 1, :] for e in range(N_EXPERTS)]
    epg = EXPERTS_PER_GROUP
    gs = [_top2_of4(*sel_r[g * epg:(g + 1) * epg]) for g in range(N_EXPERT_GROUPS)]
    best, gid = gs[0], jnp.zeros((1, t), jnp.int32)
    for g in range(1, N_EXPERT_GROUPS):
        upd = gs[g] > best
        gid = jnp.where(upd, g, gid)
        best = jnp.where(upd, gs[g], best)
    cand_sel, cand_sc = [], []
    for k in range(epg):
        a, b = sel_r[k], sc_r[k]
        for g in range(1, N_EXPERT_GROUPS):
            a = jnp.where(gid == g, sel_r[g * epg + k], a)
            b = jnp.where(gid == g, sc_r[g * epg + k], b)
        cand_sel.append(a)
        cand_sc.append(b)
    _, i1 = _first_argmax4(cand_sel)
    rest = [jnp.where(i1 == k, -jnp.inf, cand_sel[k]) for k in range(epg)]
    _, i2 = _first_argmax4(rest)
    w1 = sum(jnp.where(i1 == k, cand_sc[k], 0.0) for k in range(epg))
    w2 = sum(jnp.where(i2 == k, cand_sc[k], 0.0) for k in range(epg))
    tot = w1 + w2
    w1, w2 = w1 / tot, w2 / tot
    gate_rows = [jnp.where(i1 == k, w1, 0.0) + jnp.where(i2 == k, w2, 0.0) for k in range(epg)]
    gates_t = jnp.concatenate(gate_rows + [jnp.zeros((GATE_LANES - epg, t), F32)], axis=0)
    hr_ref[:, :d] = h
    hr_ref[:, d:] = gates_t.T
    onehot = jnp.concatenate(
        [(gid == g).astype(F32) for g in range(N_EXPERT_GROUPS)]
        + [jnp.zeros((SUBLANES - N_EXPERT_GROUPS, t), F32)], axis=0)
    before = (lax.broadcasted_iota(jnp.int32, (t, t), 0)
              < lax.broadcasted_iota(jnp.int32, (t, t), 1)).astype(BF16)
    rank_in_tile = _dot(onehot.astype(BF16), before)
    carry = carry_ref[...]
    rank = jnp.sum(onehot * (rank_in_tile + carry[:, :1]), axis=0, keepdims=True)
    gid_ref[...] = gid
    rank_ref[...] = rank.astype(jnp.int32)
    carry = carry + jnp.sum(onehot, axis=1, keepdims=True)
    carry_ref[...] = carry
    cnt_ref[...] = carry.astype(jnp.int32)


def _router_call(grp, x, layer, p):
    d = x.shape[1]
    per_token = pl.BlockSpec((1, grp.tile), lambda i: (0, i))
    return pl.pallas_call(
        _router_kernel,
        out_shape=(jax.ShapeDtypeStruct((grp.n, d + GATE_LANES), F32),
                   jax.ShapeDtypeStruct((1, grp.n), jnp.int32),
                   jax.ShapeDtypeStruct((1, grp.n), jnp.int32),
                   jax.ShapeDtypeStruct((SUBLANES, LANES), jnp.int32)),
        grid=(grp.n_tiles,),
        in_specs=[grp.row_spec(d), grp.mod_spec(layer, 3), grp.mod_spec(layer, 4),
                  _full_spec((N_EXPERTS, d)), _full_spec((N_EXPERTS, 1))],
        out_specs=(grp.row_spec(d + GATE_LANES), per_token, per_token, _full_spec((SUBLANES, LANES))),
        scratch_shapes=[pltpu.VMEM((SUBLANES, LANES), F32)],
        compiler_params=_cparams(1),
        name="moe_router",
    )(x, grp.mods, grp.mods, p["router_wt"], p["router_b"])


def _row_of(ref, row):
    if isinstance(row, tuple):
        return ref.at[row[0], pl.ds(row[1], 1), :]
    return ref.at[lax.shift_right_logical(row, 3), pl.ds(row & (SUBLANES - 1), 1), :]


def _scatter_kernel(pos_ref, gap_ref, hr_ref, out_ref, zero_ref, sem, zsem, *, tile):
    i = pl.program_id(0)
    base = i * tile

    def issue(c, carry):
        for u in range(SUBLANES):
            dst = _row_of(out_ref, pos_ref[base + c * SUBLANES + u])
            pltpu.make_async_copy(_row_of(hr_ref, (c, u)), dst, sem).start()
        return carry

    lax.fori_loop(0, tile // SUBLANES, issue, 0)

    @pl.when(i == pl.num_programs(0) - 1)
    def _():
        zero_ref[...] = jnp.zeros_like(zero_ref)
        for g in range(N_EXPERT_GROUPS):
            lo, hi = gap_ref[g], gap_ref[N_EXPERT_GROUPS + g]

            def zero_copy(r):
                return pltpu.make_async_copy(zero_ref.at[pl.ds(0, 1), :], _row_of(out_ref, r), zsem)

            def zissue(r, c):
                zero_copy(r).start()
                return c

            def zwait(r, c):
                zero_copy(r).wait()
                return c

            lax.fori_loop(lo, hi, zissue, 0)
            lax.fori_loop(lo, hi, zwait, 0)

    pltpu.make_async_copy(hr_ref, out_ref.at[pl.ds(0, tile // SUBLANES)], sem).wait()


def _scatter_call(grp, hr, pos, gaps, n_rows_out):
    w = hr.shape[1]
    grid_spec = pltpu.PrefetchScalarGridSpec(
        num_scalar_prefetch=2,
        grid=(grp.n_tiles,),
        in_specs=[pl.BlockSpec((grp.tile // SUBLANES, SUBLANES, w), lambda i, *_: (i, 0, 0))],
        out_specs=pl.BlockSpec(memory_space=pl.ANY),
        scratch_shapes=[pltpu.VMEM((SUBLANES, w), F32), pltpu.SemaphoreType.DMA(()),
                        pltpu.SemaphoreType.DMA(())],
    )
    out = pl.pallas_call(
        functools.partial(_scatter_kernel, tile=grp.tile),
        out_shape=jax.ShapeDtypeStruct((n_rows_out // SUBLANES, SUBLANES, w), F32),
        grid_spec=grid_spec,
        compiler_params=_cparams(1),
        name="moe_scatter",
    )(pos, gaps, hr.reshape(grp.n // SUBLANES, SUBLANES, w))
    return out.reshape(n_rows_out, w)


def _experts_kernel(grp_ref, na_ref, rows_ref, wg_ref, wu_ref, wd_ref, o_ref):
    j = pl.program_id(0)

    @pl.when(j >= na_ref[0])
    def _():
        o_ref[...] = jnp.zeros_like(o_ref)

    @pl.when(j < na_ref[0])
    def _():
        d = o_ref.shape[1]
        hb = rows_ref[:, :d].astype(BF16)
        acts = []
        for e in range(EXPERTS_PER_GROUP):
            g = _dot(hb, wg_ref[e])
            u = _dot(hb, wu_ref[e])
            acts.append((_silu(g) * u * rows_ref[:, d + e:d + e + 1]).astype(BF16))
        a = jnp.concatenate(acts, axis=1)
        f = wd_ref.shape[1]
        o_ref[...] = _dot(a, wd_ref[...].reshape(EXPERTS_PER_GROUP * f, d))


def _experts_call(rows, tile_grp, n_active, layer, p, expert_tile):
    n_rows, w = rows.shape
    _, _, epg, d, f = p["moe_w_gate"].shape
    grid_spec = pltpu.PrefetchScalarGridSpec(
        num_scalar_prefetch=2,
        grid=(n_rows // expert_tile,),
        in_specs=[
            pl.BlockSpec((expert_tile, w), lambda j, g, na: (jnp.minimum(j, na[0] - 1), 0)),
            pl.BlockSpec((None, None, epg, d, f), lambda j, g, na: (layer, g[j], 0, 0, 0)),
            pl.BlockSpec((None, None, epg, d, f), lambda j, g, na: (layer, g[j], 0, 0, 0)),
            pl.BlockSpec((None, None, epg, f, d), lambda j, g, na: (layer, g[j], 0, 0, 0)),
        ],
        out_specs=pl.BlockSpec((expert_tile, d), lambda j, g, na: (j, 0)),
    )
    return pl.pallas_call(
        _experts_kernel,
        out_shape=jax.ShapeDtypeStruct((n_rows, d), F32),
        grid_spec=grid_spec,
        compiler_params=_cparams(1),
        name="moe_experts",
    )(tile_grp, n_active, rows, p["moe_w_gate"], p["moe_w_up"], p["moe_w_down"])


def _experts_precise_kernel(grp_ref, na_ref, rows_ref, wg_ref, wu_ref, wd_ref, o_ref):
    j = pl.program_id(0)
    e = pl.program_id(1)

    @pl.when(e == 0)
    def _():
        o_ref[...] = jnp.zeros_like(o_ref)

    @pl.when(j < na_ref[0])
    def _():
        d = o_ref.shape[1]
        h = rows_ref[:, :d]
        gate = sum(jnp.where(e == k, rows_ref[:, d + k:d + k + 1], 0.0) for k in range(EXPERTS_PER_GROUP))
        a = _silu(_dot3(h, wg_ref[...])) * _dot3(h, wu_ref[...]) * gate
        o_ref[...] += _dot3(a, wd_ref[...])


def _experts_precise_call(rows, tile_grp, n_active, layer, p, expert_tile):
    n_rows, w = rows.shape
    _, _, epg, d, f = p["moe_w_gate"].shape

    def w_index(j, k, g, na):
        return (layer, g[j], jnp.where(j < na[0], k, epg - 1), 0, 0)

    grid_spec = pltpu.PrefetchScalarGridSpec(
        num_scalar_prefetch=2,
        grid=(n_rows // expert_tile, epg),
        in_specs=[
            pl.BlockSpec((expert_tile, w), lambda j, k, g, na: (jnp.minimum(j, na[0] - 1), 0)),
            pl.BlockSpec((None, None, None, d, f), w_index),
            pl.BlockSpec((None, None, None, d, f), w_index),
            pl.BlockSpec((None, None, None, f, d), w_index),
        ],
        out_specs=pl.BlockSpec((expert_tile, d), lambda j, k, g, na: (j, 0)),
    )
    return pl.pallas_call(
        _experts_precise_kernel,
        out_shape=jax.ShapeDtypeStruct((n_rows, d), F32),
        grid_spec=grid_spec,
        compiler_params=_cparams(2),
        name="moe_experts_precise",
    )(tile_grp, n_active, rows, p["moe_w_gate"], p["moe_w_up"], p["moe_w_down"])


def _row_plan(counts, gid, rank, expert_tile, n_tiles):
    nt = (counts + expert_tile - 1) // expert_tile
    cum = jnp.cumsum(nt)
    starts = (cum - nt) * expert_tile
    pos = starts[gid] + rank
    gap_lo = starts + counts
    gap_hi = jnp.concatenate([starts[1:], jnp.full((1,), n_tiles * expert_tile, jnp.int32)])
    n_active = cum[-1]
    j = jnp.minimum(jnp.arange(n_tiles, dtype=jnp.int32), n_active - 1)
    tile_grp = jnp.sum((j[:, None] >= cum[None, :]).astype(jnp.int32), axis=1)
    gaps = jnp.concatenate([gap_lo, gap_hi]).astype(jnp.int32)
    return pos.astype(jnp.int32), gaps, tile_grp.astype(jnp.int32), n_active.reshape(1).astype(jnp.int32)


def _moe_post_kernel(pos_ref, x_ref, gt_ref, ng_ref, nb_ref, f_ref, o_ref, buf_ref, sem, *, tile):
    i = pl.program_id(0)
    n = pl.num_programs(0)

    def issue(step, slot):
        base = step * tile

        def body(c, carry):
            for u in range(SUBLANES):
                src = _row_of(f_ref, pos_ref[base + c * SUBLANES + u])
                pltpu.make_async_copy(src, _row_of(buf_ref.at[slot], (c, u)), sem.at[slot]).start()
            return carry

        lax.fori_loop(0, tile // SUBLANES, body, 0)

    @pl.when(i == 0)
    def _():
        issue(0, 0)

    slot = i % 2

    @pl.when(i + 1 < n)
    def _():
        issue(i + 1, 1 - slot)

    pltpu.make_async_copy(f_ref.at[pl.ds(0, tile // SUBLANES)], buf_ref.at[slot], sem.at[slot]).wait()
    x = x_ref[...]
    f = buf_ref[slot].reshape(x.shape)
    o_ref[...] = _layer_norm(DEEPNORM_ALPHA * x + (1.0 + gt_ref[...]) * f, ng_ref[...], nb_ref[...])


def _moe_post_call(grp, x, layer, f_rows, pos, p):
    d = x.shape[1]
    grid_spec = pltpu.PrefetchScalarGridSpec(
        num_scalar_prefetch=1,
        grid=(grp.n_tiles,),
        in_specs=[grp.row_spec(d), grp.mod_spec(layer, 5), _at((1, d), layer, 1), _at((1, d), layer, 1),
                  pl.BlockSpec(memory_space=pl.ANY)],
        out_specs=grp.row_spec(d),
        scratch_shapes=[pltpu.VMEM((2, grp.tile // SUBLANES, SUBLANES, d), F32),
                        pltpu.SemaphoreType.DMA((2,))],
    )
    return pl.pallas_call(
        functools.partial(_moe_post_kernel, tile=grp.tile),
        out_shape=jax.ShapeDtypeStruct((grp.n, d), F32),
        grid_spec=grid_spec,
        compiler_params=_cparams(1),
        name="moe_post",
    )(pos, x, grp.mods, p["ln_g"], p["ln_b"], f_rows.reshape(f_rows.shape[0] // SUBLANES, SUBLANES, d))


def _moe_sublayer(grp, x, layer, p, expert_tile):
    hr, gid, rank, cnt = _router_call(grp, x, layer, p)
    n_tiles = grp.n // expert_tile + N_EXPERT_GROUPS
    pos, gaps, tile_grp, n_active = _row_plan(cnt[:N_EXPERT_GROUPS, 0], gid.reshape(grp.n),
                                              rank.reshape(grp.n), expert_tile, n_tiles)
    rows = _scatter_call(grp, hr, pos, gaps, n_tiles * expert_tile)
    experts = _experts_precise_call if grp.precise else _experts_call
    f_rows = experts(rows, tile_grp, n_active, layer, p, expert_tile)
    return _moe_post_call(grp, x, layer, f_rows, pos, p)


def _kv_kernel(x_ref, wk_ref, wv_ref, cos_ref, slo_ref, shi_ref, *out_refs, precise):
    x = x_ref[...]
    k = _rope(_mm(x, wk_ref, precise), cos_ref[...], slo_ref[...], shi_ref[...])
    v = _mm(x, wv_ref, precise)
    if len(out_refs) == 2:
        out_refs[0][...] = k
        out_refs[1][...] = v
    else:
        kt = k.T
        out_refs[0][...] = kt
        out_refs[1][...] = v
        out_refs[2][...] = kt.astype(BF16)
        out_refs[3][...] = v.astype(BF16)


def _rope_spec(grp, tables_rows):
    if tables_rows == 1:
        return _full_spec((1, LANES))
    tiles = tables_rows // grp.tile
    return pl.BlockSpec((grp.tile, LANES), lambda i, *_: (i % tiles, 0))


def _kv_call(grp, x, p, rope_tabs, seq):
    d = x.shape[1]
    wk_cols, wv_cols = p["kv_w_k"].shape[1], p["kv_w_v"].shape[1]
    rs = _rope_spec(grp, rope_tabs[0].shape[0])
    if seq is None:
        out_shape = [jax.ShapeDtypeStruct((grp.n, wk_cols), F32), jax.ShapeDtypeStruct((grp.n, wv_cols), F32)]
        out_specs = [grp.row_spec(wk_cols), grp.row_spec(wv_cols)]
    else:
        tiles = seq // grp.tile
        kt_spec = pl.BlockSpec((None, wk_cols, grp.tile), lambda i: (i // tiles, 0, i % tiles))
        out_shape = [jax.ShapeDtypeStruct((grp.n // seq, wk_cols, seq), F32),
                     jax.ShapeDtypeStruct((grp.n, wv_cols), F32),
                     jax.ShapeDtypeStruct((grp.n // seq, wk_cols, seq), BF16),
                     jax.ShapeDtypeStruct((grp.n, wv_cols), BF16)]
        out_specs = [kt_spec, grp.row_spec(wv_cols), kt_spec, grp.row_spec(wv_cols)]
    return pl.pallas_call(
        functools.partial(_kv_kernel, precise=grp.precise),
        out_shape=tuple(out_shape),
        grid=(grp.n_tiles,),
        in_specs=[grp.row_spec(d), _full_spec((d, wk_cols)), _full_spec((d, wv_cols)), rs, rs, rs],
        out_specs=tuple(out_specs),
        compiler_params=_cparams(1),
        name="kv_proj",
    )(x, p["kv_w_k"], p["kv_w_v"], *rope_tabs)


def _q_kernel(x_ref, sh_ref, sc_ref, wq_ref, cos_ref, slo_ref, shi_ref, o_ref, *, precise):
    h = x_ref[...] * (1.0 + sc_ref[...]) + sh_ref[...]
    q = _rope(_mm(h, wq_ref, precise), cos_ref[...], slo_ref[...], shi_ref[...])
    o_ref[...] = (q * (HEAD_DIM ** -0.5)).astype(o_ref.dtype)


def _q_call(grp, x, layer, p, rope_tabs, out_dtype):
    d = x.shape[1]
    j = layer - N_A_LAYERS
    cols = p["b_w_q"].shape[-1]
    rs = _rope_spec(grp, rope_tabs[0].shape[0])
    return pl.pallas_call(
        functools.partial(_q_kernel, precise=grp.precise),
        out_shape=jax.ShapeDtypeStruct((grp.n, cols), out_dtype),
        grid=(grp.n_tiles,),
        in_specs=[grp.row_spec(d), grp.mod_spec(layer, 0), grp.mod_spec(layer, 1),
                  _at((d, cols), j), rs, rs, rs],
        out_specs=grp.row_spec(cols),
        compiler_params=_cparams(1),
        name="q_proj",
    )(x, grp.mods, grp.mods, p["b_w_q"], *rope_tabs)


def _diff_lambda(lamv_ref, lam_init):
    lv = lamv_ref[...]
    a = jnp.exp(jnp.sum(lv[0:1] * lv[1:2], axis=1, keepdims=True))
    b = jnp.exp(jnp.sum(lv[2:3] * lv[3:4], axis=1, keepdims=True))
    return a - b + lam_init


def _attn_prompt_kernel(q_ref, kt_ref, v_ref, lamv_ref, sg_ref, o_ref, s_ref, p_ref, v1_ref,
                        *, lam_init, t):
    seq = q_ref.shape[0]
    rows_per_chunk = SOFTMAX_ROWS
    v1_ref[:, :V_DIM] = v_ref[...]
    v1_ref[:, V_DIM:] = jnp.ones((seq, V_DIM), BF16)
    lam = _diff_lambda(lamv_ref, lam_init)
    out_scale = sg_ref[...] * (1.0 - lam_init)
    lane = lax.broadcasted_iota(jnp.int32, (t, V_DIM), 1)
    row_i = lax.broadcasted_iota(jnp.int32, (rows_per_chunk, LANES), 0)
    col_i = lax.broadcasted_iota(jnp.int32, (rows_per_chunk, LANES), 1)
    for qi in range(seq // t):
        slot = qi % 2
        kend = (qi + 1) * t
        q = q_ref[qi * t:(qi + 1) * t, :]
        zero = jnp.zeros_like(q)
        qq = jnp.concatenate([jnp.where(lane < HEAD_DIM, q, zero), jnp.where(lane >= HEAD_DIM, q, zero)], axis=0)
        s_ref[slot, :, :kend] = _dot(qq, kt_ref[:, :kend])
        for rc in range(2 * t // rows_per_chunk):
            r0 = rc * rows_per_chunk
            q0 = r0 % t
            rows = slice(r0, r0 + rows_per_chunk)

            def load(cb):
                sc = s_ref[slot, rows, cb * LANES:(cb + 1) * LANES]
                c0 = cb * LANES - qi * t
                if c0 + LANES - 1 > q0:
                    sc = jnp.where(col_i + c0 <= row_i + q0, sc, NEG_BIG)
                return sc

            n_cb = min(kend, qi * t + q0 + rows_per_chunk + LANES - 1) // LANES
            n_cb = min(n_cb, kend // LANES)
            m = load(0)
            for cb in range(1, n_cb):
                m = jnp.maximum(m, load(cb))
            m = jnp.broadcast_to(jnp.max(m, axis=1, keepdims=True), m.shape)
            for cb in range(kend // LANES):
                cols = slice(cb * LANES, (cb + 1) * LANES)
                if cb < n_cb:
                    p_ref[slot, rows, cols] = jnp.exp(load(cb) - m).astype(BF16)
                else:
                    p_ref[slot, rows, cols] = jnp.zeros((rows_per_chunk, LANES), BF16)
        acc = _dot(p_ref[slot, :, :kend], v1_ref[:kend, :])
        o = acc[:, :V_DIM] / acc[:, V_DIM:]
        od = o[:t] - lam * o[t:]
        ms = jnp.mean(od * od, axis=-1, keepdims=True)
        o_ref[qi * t:(qi + 1) * t, :] = (od * lax.rsqrt(ms + LN_EPS) * out_scale).astype(o_ref.dtype)


def _attn_prompt_call(q, kt, v, layer, p, lam_init, batch, seq, t):
    n, cols = q.shape
    j = layer - N_A_LAYERS
    per_head = pl.BlockSpec((seq, V_DIM), lambda b, h: (b, h))
    return pl.pallas_call(
        functools.partial(_attn_prompt_kernel, lam_init=lam_init, t=t),
        out_shape=jax.ShapeDtypeStruct((n, cols), BF16),
        grid=(batch, N_HEADS),
        in_specs=[per_head, pl.BlockSpec((None, V_DIM, seq), lambda b, h: (b, h, 0)), per_head,
                  _at((4, HEAD_DIM), j), _at((1, V_DIM), j)],
        out_specs=per_head,
        scratch_shapes=[pltpu.VMEM((2, 2 * t, seq), F32), pltpu.VMEM((2, 2 * t, seq), BF16),
                        pltpu.VMEM((seq, 2 * V_DIM), BF16)],
        compiler_params=_cparams(2),
        name="attn_prompt",
    )(q, kt, v, p["lamv"], p["subln_g"])


def _attn_sample_kernel(pt_ref, q_ref, kn_ref, vn_ref, lamv_ref, sg_ref, *rest, lam_init):
    del pt_ref
    npg = PAGES_PER_STEP
    k_refs, v_refs = rest[:npg], rest[npg:2 * npg]
    o_ref, qcol_ref, m_ref, l_ref, acc_ref = rest[2 * npg:]
    step = pl.program_id(1)
    n_rows = 2 * N_HEADS
    cols = q_ref.shape[1]

    def seg_sums(prod):
        return jnp.sum(prod.reshape(n_rows, HEAD_DIM, prod.shape[1]), axis=1)

    def as_column(row_vec):
        return jnp.transpose(jnp.broadcast_to(row_vec, (LANES, cols)))

    @pl.when(step == 0)
    def _():
        qcol = as_column(q_ref[...])
        qcol_ref[...] = qcol
        m_ref[...] = seg_sums(qcol * as_column(kn_ref[...]))[:, :1]
        l_ref[...] = jnp.ones_like(l_ref)
        vn = vn_ref[...]
        acc_ref[...] = jnp.concatenate(
            [jnp.broadcast_to(vn[:, h * V_DIM:(h + 1) * V_DIM], (2, V_DIM)) for h in range(N_HEADS)], axis=0)

    def segment_scores(r):
        f0 = r * HEAD_DIM
        qseg = qcol_ref[f0:f0 + HEAD_DIM, :]
        return jnp.concatenate(
            [jnp.sum(k_ref[f0:f0 + HEAD_DIM, :] * qseg, axis=0, keepdims=True) for k_ref in k_refs], axis=1)

    s = jnp.concatenate([segment_scores(r) for r in range(n_rows)], axis=0)
    m_old = m_ref[...]
    m_new = jnp.maximum(m_old, jnp.max(s, axis=1, keepdims=True))
    a = jnp.exp(m_old - m_new)
    p = jnp.exp(s - m_new)
    l_ref[...] = a * l_ref[...] + jnp.sum(p, axis=1, keepdims=True)
    m_ref[...] = m_new
    p_hi, p_lo = _split_bf16(p)
    prow = lax.broadcasted_iota(jnp.int32, p.shape, 0)
    zero = jnp.zeros((), BF16)
    pv = jnp.zeros(acc_ref.shape, F32)
    for h in range(N_HEADS):
        vh = jnp.concatenate([r[pl.ds(h, PAGE_SIZE, stride=N_HEADS), :] for r in v_refs], axis=0)
        v_hi, v_lo = _split_bf16(vh)
        mine = (prow // 2) == h
        ph_hi = jnp.where(mine, p_hi, zero)
        r = _dot(jnp.concatenate([ph_hi, jnp.where(mine, p_lo, zero)], axis=0), v_hi)
        pv = pv + r[:n_rows] + r[n_rows:] + _dot(ph_hi, v_lo)
    acc_ref[...] = a * acc_ref[...] + pv

    @pl.when(step == pl.num_programs(1) - 1)
    def _():
        o = acc_ref[...] / l_ref[...]
        lam = _diff_lambda(lamv_ref, lam_init)
        outs = []
        for h in range(N_HEADS):
            od = o[2 * h:2 * h + 1] - lam * o[2 * h + 1:2 * h + 2]
            ms = jnp.mean(od * od, axis=-1, keepdims=True)
            outs.append(od * lax.rsqrt(ms + LN_EPS) * sg_ref[...] * (1.0 - lam_init))
        o_ref[...] = jnp.concatenate(outs, axis=1)


def _attn_sample_call(q, k_new, v_new, cache_kt, cache_v2, page_table, layer, p, lam_init):
    n_seq, cols = q.shape
    n_pages = page_table.shape[1]
    npg = PAGES_PER_STEP
    j = layer - N_A_LAYERS
    row3 = lambda a: a.reshape(n_seq, 1, cols)
    row_spec = pl.BlockSpec((None, 1, cols), lambda b, g, pt: (b, 0, 0))

    def page_spec(i, shape):
        return pl.BlockSpec((None,) + shape, lambda b, g, pt: (pt[b * n_pages + g * npg + i], 0, 0))

    grid_spec = pltpu.PrefetchScalarGridSpec(
        num_scalar_prefetch=1,
        grid=(n_seq, n_pages // npg),
        in_specs=[row_spec, row_spec, row_spec, _at((4, HEAD_DIM), j), _at((1, V_DIM), j)]
        + [page_spec(i, cache_kt.shape[1:]) for i in range(npg)]
        + [page_spec(i, cache_v2.shape[1:]) for i in range(npg)],
        out_specs=row_spec,
        scratch_shapes=[pltpu.VMEM((cols, LANES), F32), pltpu.VMEM((2 * N_HEADS, 1), F32),
                        pltpu.VMEM((2 * N_HEADS, 1), F32), pltpu.VMEM((2 * N_HEADS, V_DIM), F32)],
    )
    out = pl.pallas_call(
        functools.partial(_attn_sample_kernel, lam_init=lam_init),
        out_shape=jax.ShapeDtypeStruct((n_seq, 1, cols), F32),
        grid_spec=grid_spec,
        compiler_params=_cparams(2),
        name="attn_sample",
    )(page_table.reshape(-1), row3(q), row3(k_new), row3(v_new), p["lamv"], p["subln_g"],
      *([cache_kt] * npg), *([cache_v2] * npg))
    return out.reshape(n_seq, cols)


def _attn_out_kernel(o_ref, x_ref, gt_ref, wo_ref, ng_ref, nb_ref, y_ref, *, precise):
    y = _mm(o_ref[...], wo_ref, precise)
    x = x_ref[...]
    y_ref[...] = _layer_norm(DEEPNORM_ALPHA * x + (1.0 + gt_ref[...]) * y, ng_ref[...], nb_ref[...])


def _attn_out_call(grp, o, x, layer, p):
    d = x.shape[1]
    cols = o.shape[1]
    j = layer - N_A_LAYERS
    return pl.pallas_call(
        functools.partial(_attn_out_kernel, precise=grp.precise),
        out_shape=jax.ShapeDtypeStruct((grp.n, d), F32),
        grid=(grp.n_tiles,),
        in_specs=[grp.row_spec(cols), grp.row_spec(d), grp.mod_spec(layer, 2), _at((cols, d), j),
                  _at((1, d), layer, 0), _at((1, d), layer, 0)],
        out_specs=grp.row_spec(d),
        compiler_params=_cparams(1),
        name="attn_out",
    )(o, x, grp.mods, p["b_w_o"], p["ln_g"], p["ln_b"])


def _rope_tables(pos):
    half = ROT_DIM // 2
    inv = 1.0 / (ROPE_THETA ** (jnp.arange(0, ROT_DIM, 2, dtype=F32) / ROT_DIM))
    ang = pos.astype(F32)[:, None] * inv[None, :]
    cos, sin = jnp.cos(ang), jnp.sin(ang)
    n = pos.shape[0]
    ones = jnp.ones((n, HEAD_DIM - ROT_DIM), F32)
    zeros = jnp.zeros((n, HEAD_DIM - ROT_DIM), F32)
    zh = jnp.zeros((n, half), F32)
    cos_t = jnp.concatenate([cos, cos, ones], axis=1)
    sin_lo = jnp.concatenate([-sin, zh, zeros], axis=1)
    sin_hi = jnp.concatenate([zh, sin, zeros], axis=1)
    rep = LANES // HEAD_DIM
    return tuple(jnp.tile(t, (1, rep)) for t in (cos_t, sin_lo, sin_hi))


def _trunk(grp, x, p, rope_tabs, attend, expert_tile, seq):
    v_rows = []
    k_sh = v_sh = kv_attn = None
    for l in range(DEPTH):
        if l < N_A_LAYERS:
            x, v_a = _a_mixer_call(grp, x, l, p)
            v_rows.append(v_a)
        else:
            lam_init = 0.8 - 0.6 * math.exp(-0.3 * l)
            q = _q_call(grp, x, l, p, rope_tabs, F32 if seq is None else BF16)
            o = attend(q, kv_attn, l, p, lam_init)
            x = _attn_out_call(grp, o, x, l, p)
        x = _moe_sublayer(grp, x, l, p, expert_tile)
        if l == N_A_LAYERS - 1:
            outs = _kv_call(grp, x, p, rope_tabs, seq)
            k_sh, v_sh = outs[0], outs[1]
            kv_attn = (k_sh, v_sh) if seq is None else (outs[2], outs[3])
    return x, k_sh, v_sh, v_rows


def kernel(x_prompt, x_sample, cache_k, cache_v, page_table, c_prompt, c_sample, ada_w, ada_b, ln_g, ln_b, a_w_in, a_ln_g, a_ln_b, a_w_s, a_b_s, a_w_out, kv_w_k, kv_w_v, b_w_q, b_lam_q1, b_lam_k1, b_lam_q2, b_lam_k2, b_subln_g, b_w_o, router_w, router_bias, moe_w_gate, moe_w_up, moe_w_down):
    batch, seq, d = x_prompt.shape
    dec_batch, dec_seq, _ = x_sample.shape
    assert dec_seq == 1, "the sample group is one new token per sequence"
    n_pool = cache_k.shape[0]
    past = page_table.shape[1] * PAGE_SIZE
    depth, n_exp, _, d_exp = moe_w_gate.shape
    n_a, d_a = a_ln_g.shape
    n_b = b_w_q.shape[0]
    gd = d_a // A_GROUPS
    epg = EXPERTS_PER_GROUP

    p_f32 = dict(
        a_w_in=a_w_in, a_w_out=a_w_out, kv_w_k=kv_w_k, kv_w_v=kv_w_v, b_w_q=b_w_q, b_w_o=b_w_o,
        moe_w_gate=moe_w_gate.reshape(depth, n_exp // epg, epg, d, d_exp),
        moe_w_up=moe_w_up.reshape(depth, n_exp // epg, epg, d, d_exp),
        moe_w_down=moe_w_down.reshape(depth, n_exp // epg, epg, d_exp, d),
        a_ln_g=a_ln_g.reshape(n_a, 1, d_a), a_ln_b=a_ln_b.reshape(n_a, 1, d_a),
        a_w_s=a_w_s, a_b_s_t=jnp.swapaxes(a_b_s, 1, 2),
        a_coef=jnp.repeat(a_w_s[:, :, 0, 0], gd, axis=1).reshape(n_a, 1, d_a),
        a_bias=jnp.repeat(a_b_s[:, :, 0], gd, axis=1).reshape(n_a, 1, d_a),
        ln_g=ln_g.reshape(depth, 2, 1, d), ln_b=ln_b.reshape(depth, 2, 1, d),
        lamv=jnp.stack([b_lam_q1, b_lam_k1, b_lam_q2, b_lam_k2], axis=1),
        subln_g=b_subln_g.reshape(n_b, 1, V_DIM),
        router_wt=router_w.T, router_b=router_bias.reshape(N_EXPERTS, 1),
    )
    matmul_weights = ("a_w_in", "a_w_out", "kv_w_k", "kv_w_v", "b_w_q", "b_w_o",
                      "moe_w_gate", "moe_w_up", "moe_w_down")
    p_bf16 = {k: (v.astype(BF16) if k in matmul_weights else v) for k, v in p_f32.items()}

    mods = _ada_call(jnp.concatenate([c_sample, c_prompt], axis=0), ada_w, ada_b)

    gp = _Group(batch * seq, PROMPT_TILE, seq, dec_batch, mods)
    tabs_p = _rope_tables(jnp.arange(seq, dtype=jnp.int32))

    def attend_prompt(q, kv, layer, p, lam_init):
        return _attn_prompt_call(q, kv[0], kv[1], layer, p, lam_init, batch, seq, ATTN_TILE)

    y_p, kt_p, v_p, _ = _trunk(gp, x_prompt.reshape(batch * seq, d), p_bf16, tabs_p, attend_prompt,
                               PROMPT_EXPERT_TILE, seq)
    k_p = jnp.transpose(kt_p.reshape(batch, N_HEADS, 2, HEAD_DIM, seq), (0, 4, 1, 2, 3))

    gs = _Group(dec_batch, dec_batch, 1, 0, mods)
    tabs_s = _rope_tables(jnp.full((1,), past, dtype=jnp.int32))
    cache_kt = jnp.transpose(cache_k, (0, 2, 3, 4, 1)).reshape(n_pool, N_HEADS * 2 * HEAD_DIM, PAGE_SIZE)
    cache_v2 = cache_v.reshape(n_pool, PAGE_SIZE * N_HEADS, V_DIM)

    def attend_sample(q, kv, layer, p, lam_init):
        return _attn_sample_call(q, kv[0], kv[1], cache_kt, cache_v2, page_table, layer, p, lam_init)

    y_s, k_s, v_s, v_rows = _trunk(gs, x_sample.reshape(dec_batch, d), p_f32, tabs_s, attend_sample,
                                   dec_batch, None)

    return (
        y_p.reshape(batch, seq, d),
        y_s.reshape(dec_batch, 1, d),
        k_p,
        v_p.reshape(batch, seq, N_HEADS, V_DIM),
        k_s.reshape(dec_batch, 1, N_HEADS, 2, HEAD_DIM),
        v_s.reshape(dec_batch, 1, N_HEADS, V_DIM),
        jnp.stack(v_rows, axis=0).reshape(N_A_LAYERS, dec_batch, 1, d_a),
    )
```

```python
import functools
import math

import jax
import jax.numpy as jnp
from jax import lax
from jax.experimental import pallas as pl
from jax.experimental.pallas import tpu as pltpu

DEPTH = 4
N_A_LAYERS = DEPTH // 2
CHUNK = 128
A_GROUPS = 8
N_HEADS = 8
HEAD_DIM = 64
V_DIM = 2 * HEAD_DIM
ROT_DIM = HEAD_DIM // 4
ROPE_THETA = 500000.0
N_EXPERTS = 16
N_EXPERT_GROUPS = 4
EXPERTS_PER_GROUP = N_EXPERTS // N_EXPERT_GROUPS
LN_EPS = 1e-5
DEEPNORM_ALPHA = (2 * DEPTH) ** 0.25
PAGE_SIZE = 128

LANES = 128
SUBLANES = 8
BF16_SUBLANES = 16
VMEM_LIMIT_BYTES = 56 * 1024 * 1024

GATE_LANES = LANES
NEG_BIG = -1e30
PAGES_PER_STEP = 16
SOFTMAX_ROWS = 32

PROMPT_TILE = 512
PROMPT_EXPERT_TILE = 512
ATTN_TILE = 256

F32 = jnp.float32
BF16 = jnp.bfloat16


def _cparams(n_axes):
    return pltpu.CompilerParams(
        dimension_semantics=("arbitrary",) * n_axes, vmem_limit_bytes=VMEM_LIMIT_BYTES)


def _dot(a, b):
    return jnp.dot(a, b, preferred_element_type=F32)


def _dot_nt(a, b):
    return lax.dot_general(a, b, (((1,), (1,)), ((), ())), preferred_element_type=F32)


def _split_bf16(a):
    hi = a.astype(BF16)
    return hi, (a - hi.astype(F32)).astype(BF16)


def _dot3(x, w):
    x_hi, x_lo = _split_bf16(x)
    w_hi, w_lo = _split_bf16(w)
    m = x.shape[0]
    if m % BF16_SUBLANES == 0:
        r = _dot(jnp.concatenate([x_hi, x_lo], axis=0), w_hi)
        return r[:m] + r[m:] + _dot(x_hi, w_lo)
    return _dot(x_hi, w_hi) + _dot(x_lo, w_hi) + _dot(x_hi, w_lo)


def _mm(x, w_ref, precise):
    if precise:
        return _dot3(x, w_ref[...])
    return _dot(x.astype(BF16), w_ref[...])


def _layer_norm(x, g, b):
    mu = jnp.mean(x, axis=-1, keepdims=True)
    xc = x - mu
    var = jnp.mean(xc * xc, axis=-1, keepdims=True)
    return xc * lax.rsqrt(var + LN_EPS) * g + b


def _silu(x):
    return x * (1.0 / (1.0 + jnp.exp(-x)))


def _gelu_tanh(x):
    return x * (0.5 * (1.0 + jnp.tanh(0.7978845608028654 * (x + 0.044715 * (x * x * x)))))


def _tile_lanes(t, reps):
    return jnp.concatenate([t] * reps, axis=1)


def _rope(x, cos_t, sin_lo, sin_hi):
    reps = x.shape[1] // LANES
    n = x.shape[1]
    nxt = pltpu.roll(x, n - ROT_DIM // 2, axis=1)
    prv = pltpu.roll(x, ROT_DIM // 2, axis=1)
    return (x * _tile_lanes(cos_t, reps) + nxt * _tile_lanes(sin_lo, reps)
            + prv * _tile_lanes(sin_hi, reps))


def _at(block, *lead):
    nd = len(block)
    return pl.BlockSpec((None,) * len(lead) + tuple(block), lambda *_: tuple(lead) + (0,) * nd)


def _full_spec(shape):
    return _at(shape)


def _ada_kernel(c_ref, w_ref, b_ref, o_ref):
    o_ref[...] = _dot3(_silu(c_ref[...]), w_ref[...]) + b_ref[...]


def _ada_call(c_all, ada_w, ada_b):
    depth, d, d6 = ada_w.shape
    r = c_all.shape[0]
    n_mod = d6 // d
    return pl.pallas_call(
        _ada_kernel,
        out_shape=jax.ShapeDtypeStruct((depth, n_mod, r, d), F32),
        grid=(depth, n_mod),
        in_specs=[
            pl.BlockSpec((r, d), lambda l, j: (0, 0)),
            pl.BlockSpec((None, d, d), lambda l, j: (l, 0, j)),
            pl.BlockSpec((None, None, 1, d), lambda l, j: (l, j, 0, 0)),
        ],
        out_specs=pl.BlockSpec((None, None, r, d), lambda l, j: (l, j, 0, 0)),
        compiler_params=_cparams(2),
        name="ada_mod",
    )(c_all, ada_w, ada_b.reshape(depth, n_mod, 1, d))


class _Group:
    def __init__(self, n_rows, tile, rows_per_cond, cond_row0, mods):
        self.n = n_rows
        self.tile = tile
        self.n_tiles = n_rows // tile
        self.per_row = rows_per_cond == 1
        self.precise = self.per_row
        self.mods = mods if self.per_row else mods.reshape(mods.shape[:3] + (1, mods.shape[3]))
        self.tiles_per_cond = None if self.per_row else rows_per_cond // tile
        self.cond_row0 = cond_row0
        self.d = mods.shape[-1]

    def mod_spec(self, layer, j):
        if self.per_row:
            return pl.BlockSpec((None, None, self.tile, self.d), lambda i, *_: (layer, j, 0, 0))
        tpc, r0 = self.tiles_per_cond, self.cond_row0
        return pl.BlockSpec((None, None, None, 1, self.d),
                            lambda i, *_: (layer, j, r0 + i // tpc, 0, 0))

    def row_spec(self, width):
        return pl.BlockSpec((self.tile, width), lambda i, *_: (i, 0))


def _a_mixer_prompt_kernel(x_ref, sh_ref, sc_ref, gt_ref, win_ref, lng_ref, lnb_ref, ws_ref,
                           bst_ref, wout_ref, ng_ref, nb_ref, o_ref):
    x = x_ref[...]
    t = x.shape[0]
    d_a = lng_ref.shape[1]
    h = (x * (1.0 + sc_ref[...]) + sh_ref[...]).astype(BF16)
    z = _gelu_tanh(_dot(h, win_ref[...]))
    u = z[:, :d_a]
    v = _layer_norm(z[:, d_a:], lng_ref[...], lnb_ref[...]).astype(BF16)
    causal = (lax.broadcasted_iota(jnp.int32, (CHUNK, CHUNK), 0)
              >= lax.broadcasted_iota(jnp.int32, (CHUNK, CHUNK), 1))
    gd = d_a // A_GROUPS
    cols = []
    for g in range(A_GROUPS):
        wsg = jnp.where(causal, ws_ref[g], 0.0).astype(BF16)
        bias = bst_ref[:, g:g + 1]
        rows = [_dot(wsg, v[c * CHUNK:(c + 1) * CHUNK, g * gd:(g + 1) * gd]) + bias
                for c in range(t // CHUNK)]
        cols.append(jnp.concatenate(rows, axis=0))
    s = jnp.concatenate(cols, axis=1)
    y = _dot((u * s).astype(BF16), wout_ref[...])
    o_ref[...] = _layer_norm(DEEPNORM_ALPHA * x + (1.0 + gt_ref[...]) * y, ng_ref[...], nb_ref[...])


def _a_mixer_sample_kernel(x_ref, sh_ref, sc_ref, gt_ref, win_ref, lng_ref, lnb_ref, coef_ref,
                           bias_ref, wout_ref, ng_ref, nb_ref, o_ref, v_ref):
    x = x_ref[...]
    d_a = lng_ref.shape[1]
    h = x * (1.0 + sc_ref[...]) + sh_ref[...]
    z = _gelu_tanh(_dot3(h, win_ref[...]))
    u = z[:, :d_a]
    v = _layer_norm(z[:, d_a:], lng_ref[...], lnb_ref[...])
    v_ref[...] = v
    s = v * coef_ref[...] + bias_ref[...]
    y = _dot3(u * s, wout_ref[...])
    o_ref[...] = _layer_norm(DEEPNORM_ALPHA * x + (1.0 + gt_ref[...]) * y, ng_ref[...], nb_ref[...])


def _a_mixer_call(grp, x, layer, p):
    d = x.shape[1]
    d_a = p["a_ln_g"].shape[-1]
    common_in = [
        grp.row_spec(d), grp.mod_spec(layer, 0), grp.mod_spec(layer, 1), grp.mod_spec(layer, 2),
        _at((d, 2 * d_a), layer), _at((1, d_a), layer), _at((1, d_a), layer),
    ]
    tail_in = [_at((d_a, d), layer), _at((1, d), layer, 0), _at((1, d), layer, 0)]
    args_head = (x, grp.mods, grp.mods, grp.mods, p["a_w_in"], p["a_ln_g"], p["a_ln_b"])
    args_tail = (p["a_w_out"], p["ln_g"], p["ln_b"])
    if grp.per_row:
        return pl.pallas_call(
            _a_mixer_sample_kernel,
            out_shape=(jax.ShapeDtypeStruct((grp.n, d), F32), jax.ShapeDtypeStruct((grp.n, d_a), F32)),
            grid=(grp.n_tiles,),
            in_specs=common_in + [_at((1, d_a), layer), _at((1, d_a), layer)] + tail_in,
            out_specs=(grp.row_spec(d), grp.row_spec(d_a)),
            compiler_params=_cparams(1),
            name="a_mixer_sample",
        )(*args_head, p["a_coef"], p["a_bias"], *args_tail)
    out = pl.pallas_call(
        _a_mixer_prompt_kernel,
        out_shape=jax.ShapeDtypeStruct((grp.n, d), F32),
        grid=(grp.n_tiles,),
        in_specs=common_in + [_at((A_GROUPS, CHUNK, CHUNK), layer), _at((CHUNK, A_GROUPS), layer)] + tail_in,
        out_specs=grp.row_spec(d),
        compiler_params=_cparams(1),
        name="a_mixer_prompt",
    )(*args_head, p["a_w_s"], p["a_b_s_t"], *args_tail)
    return out, None


def _top2_of4(a, b, c, d):
    hi1, lo1 = jnp.maximum(a, b), jnp.minimum(a, b)
    hi2, lo2 = jnp.maximum(c, d), jnp.minimum(c, d)
    return jnp.maximum(hi1, hi2) + jnp.maximum(jnp.minimum(hi1, hi2), jnp.maximum(lo1, lo2))


def _first_argmax4(vals):
    m = jnp.maximum(jnp.maximum(vals[0], vals[1]), jnp.maximum(vals[2], vals[3]))
    idx = jnp.where(vals[0] == m, 0, jnp.where(vals[1] == m, 1, jnp.where(vals[2] == m, 2, 3)))
    return m, idx


def _route(h, rwt_ref, rb_ref):
    t = h.shape[0]
    logits = lax.dot_general(rwt_ref[...], h, (((1,), (1,)), ((), ())),
                             precision=lax.Precision.HIGHEST, preferred_element_type=F32)
    mx = jnp.max(logits, axis=0, keepdims=True)
    ex = jnp.exp(logits - mx)
    scores = ex / jnp.sum(ex, axis=0, keepdims=True)
    sel = scores + rb_ref[...]
    sel_r = [sel[e:e + 1, :] for e in range(N_EXPERTS)]
    sc_r = [scores[e:e + 1, :] for e in range(N_EXPERTS)]
    epg = EXPERTS_PER_GROUP
    gs = [_top2_of4(*sel_r[g * epg:(g + 1) * epg]) for g in range(N_EXPERT_GROUPS)]
    best, gid = gs[0], jnp.zeros((1, t), jnp.int32)
    for g in range(1, N_EXPERT_GROUPS):
        upd = gs[g] > best
        gid = jnp.where(upd, g, gid)
        best = jnp.where(upd, gs[g], best)
    cand_sel, cand_sc = [], []
    for k in range(epg):
        a, b = sel_r[k], sc_r[k]
        for g in range(1, N_EXPERT_GROUPS):
            a = jnp.where(gid == g, sel_r[g * epg + k], a)
            b = jnp.where(gid == g, sc_r[g * epg + k], b)
        cand_sel.append(a)
        cand_sc.append(b)
    _, i1 = _first_argmax4(cand_sel)
    rest = [jnp.where(i1 == k, -jnp.inf, cand_sel[k]) for k in range(epg)]
    _, i2 = _first_argmax4(rest)
    w1 = sum(jnp.where(i1 == k, cand_sc[k], 0.0) for k in range(epg))
    w2 = sum(jnp.where(i2 == k, cand_sc[k], 0.0) for k in range(epg))
    tot = w1 + w2
    w1, w2 = w1 / tot, w2 / tot
    gate_rows = [jnp.where(i1 == k, w1, 0.0) + jnp.where(i2 == k, w2, 0.0) for k in range(epg)]
    gates_t = jnp.concatenate(gate_rows + [jnp.zeros((GATE_LANES - epg, t), F32)], axis=0)
    onehot = jnp.concatenate(
        [(gid == g).astype(F32) for g in range(N_EXPERT_GROUPS)]
        + [jnp.zeros((SUBLANES - N_EXPERT_GROUPS, t), F32)], axis=0)
    before = (lax.broadcasted_iota(jnp.int32, (t, t), 0)
              < lax.broadcasted_iota(jnp.int32, (t, t), 1)).astype(BF16)
    rank_in_tile = _dot(onehot.astype(BF16), before)
    return gid, gates_t.T, onehot, rank_in_tile


def _router_kernel(x_ref, sh_ref, sc_ref, rwt_ref, rb_ref, hr_ref, gid_ref, rank_ref, cnt_ref,
                   carry_ref):
    i = pl.program_id(0)
    d = x_ref.shape[1]

    @pl.when(i == 0)
    def _():
        carry_ref[...] = jnp.zeros_like(carry_ref)

    h = x_ref[...] * (1.0 + sc_ref[...]) + sh_ref[...]
    gid, gates, onehot, rank_in_tile = _route(h, rwt_ref, rb_ref)
    hr_ref[:, :d] = h
    hr_ref[:, d:] = gates
    carry = carry_ref[...]
    rank = jnp.sum(onehot * (rank_in_tile + carry[:, :1]), axis=0, keepdims=True)
    gid_ref[...] = gid
    rank_ref[...] = rank.astype(jnp.int32)
    carry = carry + jnp.sum(onehot, axis=1, keepdims=True)
    carry_ref[...] = carry
    cnt_ref[...] = carry.astype(jnp.int32)


def _router_sorted_kernel(x_ref, sh_ref, sc_ref, rwt_ref, rb_ref, rows_ref, gid_ref, rank_ref, cnt_ref):
    d = x_ref.shape[1]
    ts = rows_ref.shape[0]
    h = x_ref[...] * (1.0 + sc_ref[...]) + sh_ref[...]
    gid, gates, onehot, rank_in_tile = _route(h, rwt_ref, rb_ref)
    t = h.shape[0]
    cnt = jnp.sum(onehot, axis=1, keepdims=True)
    cnt8 = jnp.floor((cnt + (SUBLANES - 1)) * (1.0 / SUBLANES)) * SUBLANES
    lrank = jnp.sum(onehot * rank_in_tile, axis=0, keepdims=True)
    dest, off = lrank, jnp.zeros((1, 1), F32)
    for g in range(1, N_EXPERT_GROUPS):
        off = off + cnt8[g - 1:g, :]
        dest = dest + onehot[g:g + 1, :] * off
    place = (lax.broadcasted_iota(jnp.int32, (ts, t), 0) == dest.astype(jnp.int32)).astype(BF16)
    g_hi = gates.astype(BF16)
    g_r1 = gates - g_hi.astype(F32)
    g_mid = g_r1.astype(BF16)
    g_lo = (g_r1 - g_mid.astype(F32)).astype(BF16)
    placed = _dot(place, jnp.concatenate([h.astype(BF16), g_hi, g_mid, g_lo], axis=1))
    gl = GATE_LANES
    rows_ref[:, :d] = placed[:, :d]
    rows_ref[:, d:] = placed[:, d:d + gl] + placed[:, d + gl:d + 2 * gl] + placed[:, d + 2 * gl:]
    gid_ref[...] = gid
    rank_ref[...] = lrank.astype(jnp.int32)
    cnt_ref[...] = jnp.broadcast_to(cnt, cnt_ref.shape).astype(jnp.int32)


def _router_call(grp, x, layer, p):
    d = x.shape[1]
    per_token = pl.BlockSpec((1, grp.tile), lambda i: (0, i))
    return pl.pallas_call(
        _router_kernel,
        out_shape=(jax.ShapeDtypeStruct((grp.n, d + GATE_LANES), F32),
                   jax.ShapeDtypeStruct((1, grp.n), jnp.int32),
                   jax.ShapeDtypeStruct((1, grp.n), jnp.int32),
                   jax.ShapeDtypeStruct((SUBLANES, LANES), jnp.int32)),
        grid=(grp.n_tiles,),
        in_specs=[grp.row_spec(d), grp.mod_spec(layer, 3), grp.mod_spec(layer, 4),
                  _full_spec((N_EXPERTS, d)), _full_spec((N_EXPERTS, 1))],
        out_specs=(grp.row_spec(d + GATE_LANES), per_token, per_token, _full_spec((SUBLANES, LANES))),
        scratch_shapes=[pltpu.VMEM((SUBLANES, LANES), F32)],
        compiler_params=_cparams(1),
        name="moe_router",
    )(x, grp.mods, grp.mods, p["router_wt"], p["router_b"])


def _sorted_tile_rows(tile):
    return tile + N_EXPERT_GROUPS * SUBLANES


def _router_sorted_call(grp, x, layer, p):
    d = x.shape[1]
    ts = _sorted_tile_rows(grp.tile)
    per_token = pl.BlockSpec((1, grp.tile), lambda i: (0, i))
    return pl.pallas_call(
        _router_sorted_kernel,
        out_shape=(jax.ShapeDtypeStruct((grp.n_tiles * ts, d + GATE_LANES), F32),
                   jax.ShapeDtypeStruct((1, grp.n), jnp.int32),
                   jax.ShapeDtypeStruct((1, grp.n), jnp.int32),
                   jax.ShapeDtypeStruct((grp.n_tiles, SUBLANES, LANES), jnp.int32)),
        grid=(grp.n_tiles,),
        in_specs=[grp.row_spec(d), grp.mod_spec(layer, 3), grp.mod_spec(layer, 4),
                  _full_spec((N_EXPERTS, d)), _full_spec((N_EXPERTS, 1))],
        out_specs=(pl.BlockSpec((ts, d + GATE_LANES), lambda i: (i, 0)), per_token, per_token,
                   pl.BlockSpec((None, SUBLANES, LANES), lambda i: (i, 0, 0))),
        compiler_params=_cparams(1),
        name="moe_router_sorted",
    )(x, grp.mods, grp.mods, p["router_wt"], p["router_b"])


def _row_of(ref, row):
    if isinstance(row, tuple):
        return ref.at[row[0], pl.ds(row[1], 1), :]
    return ref.at[lax.shift_right_logical(row, 3), pl.ds(row & (SUBLANES - 1), 1), :]


def _scatter_kernel(pos_ref, gap_ref, hr_ref, out_ref, zero_ref, sem, zsem, *, tile):
    i = pl.program_id(0)
    base = i * tile

    def issue(c, carry):
        for u in range(SUBLANES):
            dst = _row_of(out_ref, pos_ref[base + c * SUBLANES + u])
            pltpu.make_async_copy(_row_of(hr_ref, (c, u)), dst, sem).start()
        return carry

    lax.fori_loop(0, tile // SUBLANES, issue, 0)

    @pl.when(i == pl.num_programs(0) - 1)
    def _():
        zero_ref[...] = jnp.zeros_like(zero_ref)
        for g in range(N_EXPERT_GROUPS):
            lo, hi = gap_ref[g], gap_ref[N_EXPERT_GROUPS + g]

            def zero_copy(r):
                return pltpu.make_async_copy(zero_ref.at[pl.ds(0, 1), :], _row_of(out_ref, r), zsem)

            def zissue(r, c):
                zero_copy(r).start()
                return c

            def zwait(r, c):
                zero_copy(r).wait()
                return c

            lax.fori_loop(lo, hi, zissue, 0)
            lax.fori_loop(lo, hi, zwait, 0)

    pltpu.make_async_copy(hr_ref, out_ref.at[pl.ds(0, tile // SUBLANES)], sem).wait()


def _scatter_call(grp, hr, pos, gaps, n_rows_out):
    w = hr.shape[1]
    grid_spec = pltpu.PrefetchScalarGridSpec(
        num_scalar_prefetch=2,
        grid=(grp.n_tiles,),
        in_specs=[pl.BlockSpec((grp.tile // SUBLANES, SUBLANES, w), lambda i, *_: (i, 0, 0))],
        out_specs=pl.BlockSpec(memory_space=pl.ANY),
        scratch_shapes=[pltpu.VMEM((SUBLANES, w), F32), pltpu.SemaphoreType.DMA(()),
                        pltpu.SemaphoreType.DMA(())],
    )
    out = pl.pallas_call(
        functools.partial(_scatter_kernel, tile=grp.tile),
        out_shape=jax.ShapeDtypeStruct((n_rows_out // SUBLANES, SUBLANES, w), F32),
        grid_spec=grid_spec,
        compiler_params=_cparams(1),
        name="moe_scatter",
    )(pos, gaps, hr.reshape(grp.n // SUBLANES, SUBLANES, w))
    return out.reshape(n_rows_out, w)


def _scatter_runs_kernel(run_ref, gap_ref, rows_ref, out_ref, zero_ref, sem, zsem, *, n_runs, max_len8):
    i = pl.program_id(0)
    n_bits = max_len8.bit_length()

    def run_copies(g):
        r = i * N_EXPERT_GROUPS + g
        src, dst, length = run_ref[r], run_ref[n_runs + r], run_ref[2 * n_runs + r]
        for b in range(n_bits):
            size = 1 << b
            off = lax.shift_left(lax.shift_right_logical(length, b + 1), b + 1)
            copy = pltpu.make_async_copy(rows_ref.at[pl.ds(src + off, size)],
                                         out_ref.at[pl.ds(dst + off, size)], sem)
            yield (lax.shift_right_logical(length, b) & 1) == 1, copy

    for g in range(N_EXPERT_GROUPS):
        for has_bit, copy in run_copies(g):
            pl.when(has_bit)(copy.start)

    @pl.when(i == pl.num_programs(0) - 1)
    def _():
        zero_ref[...] = jnp.zeros_like(zero_ref)
        for g in range(N_EXPERT_GROUPS):
            lo, hi = gap_ref[g], gap_ref[N_EXPERT_GROUPS + g]

            def zero_copy(r):
                return pltpu.make_async_copy(zero_ref, out_ref.at[pl.ds(r, 1)], zsem)

            def zissue(r, c):
                zero_copy(r).start()
                return c

            def zwait(r, c):
                zero_copy(r).wait()
                return c

            lax.fori_loop(lo, hi, zissue, 0)
            lax.fori_loop(lo, hi, zwait, 0)

    for g in range(N_EXPERT_GROUPS):
        for has_bit, copy in run_copies(g):
            pl.when(has_bit)(copy.wait)


def _scatter_runs_call(grp, rows, runs, gaps8, n_rows_out):
    w = rows.shape[1]
    ts8 = _sorted_tile_rows(grp.tile) // SUBLANES
    grid_spec = pltpu.PrefetchScalarGridSpec(
        num_scalar_prefetch=2,
        grid=(grp.n_tiles,),
        in_specs=[pl.BlockSpec((ts8, SUBLANES, w), lambda i, *_: (i, 0, 0))],
        out_specs=pl.BlockSpec(memory_space=pl.ANY),
        scratch_shapes=[pltpu.VMEM((1, SUBLANES, w), F32), pltpu.SemaphoreType.DMA(()),
                        pltpu.SemaphoreType.DMA(())],
    )
    out = pl.pallas_call(
        functools.partial(_scatter_runs_kernel, n_runs=grp.n_tiles * N_EXPERT_GROUPS, max_len8=ts8),
        out_shape=jax.ShapeDtypeStruct((n_rows_out // SUBLANES, SUBLANES, w), F32),
        grid_spec=grid_spec,
        compiler_params=_cparams(1),
        name="moe_scatter_runs",
    )(runs, gaps8, rows.reshape(rows.shape[0] // SUBLANES, SUBLANES, w))
    return out.reshape(n_rows_out, w)


def _run_plan(tile_counts, gid, lrank, tile, expert_tile, n_tiles):
    c8 = (tile_counts + SUBLANES - 1) // SUBLANES * SUBLANES
    run_src = jnp.cumsum(c8, axis=1) - c8
    rows_g = jnp.sum(c8, axis=0)
    nt = (rows_g + expert_tile - 1) // expert_tile
    cum = jnp.cumsum(nt)
    starts = (cum - nt) * expert_tile
    run_dst = starts[None, :] + jnp.cumsum(c8, axis=0) - c8
    token_tile = jnp.arange(gid.shape[0], dtype=jnp.int32) // tile
    pos = run_dst[token_tile, gid] + lrank
    gap_lo = starts + rows_g
    gap_hi = jnp.concatenate([starts[1:], jnp.full((1,), n_tiles * expert_tile, jnp.int32)])
    n_active = cum[-1]
    j = jnp.minimum(jnp.arange(n_tiles, dtype=jnp.int32), n_active - 1)
    tile_grp = jnp.sum((j[:, None] >= cum[None, :]).astype(jnp.int32), axis=1)
    runs = (jnp.concatenate([run_src.reshape(-1), run_dst.reshape(-1), c8.reshape(-1)]) // SUBLANES)
    gaps8 = jnp.concatenate([gap_lo, gap_hi]) // SUBLANES
    return (pos.astype(jnp.int32), runs.astype(jnp.int32), gaps8.astype(jnp.int32),
            tile_grp.astype(jnp.int32), n_active.reshape(1).astype(jnp.int32))


def _experts_kernel(grp_ref, na_ref, rows_ref, wg_ref, wu_ref, wd_ref, o_ref):
    j = pl.program_id(0)

    @pl.when(j >= na_ref[0])
    def _():
        o_ref[...] = jnp.zeros_like(o_ref)

    @pl.when(j < na_ref[0])
    def _():
        d = o_ref.shape[1]
        hb = rows_ref[:, :d].astype(BF16)
        acts = []
        for e in range(EXPERTS_PER_GROUP):
            g = _dot(hb, wg_ref[e])
            u = _dot(hb, wu_ref[e])
            acts.append((_silu(g) * u * rows_ref[:, d + e:d + e + 1]).astype(BF16))
        a = jnp.concatenate(acts, axis=1)
        f = wd_ref.shape[1]
        o_ref[...] = _dot(a, wd_ref[...].reshape(EXPERTS_PER_GROUP * f, d))


def _experts_call(rows, tile_grp, n_active, layer, p, expert_tile):
    n_rows, w = rows.shape
    _, _, epg, d, f = p["moe_w_gate"].shape
    grid_spec = pltpu.PrefetchScalarGridSpec(
        num_scalar_prefetch=2,
        grid=(n_rows // expert_tile,),
        in_specs=[
            pl.BlockSpec((expert_tile, w), lambda j, g, na: (jnp.minimum(j, na[0] - 1), 0)),
            pl.BlockSpec((None, None, epg, d, f), lambda j, g, na: (layer, g[j], 0, 0, 0)),
            pl.BlockSpec((None, None, epg, d, f), lambda j, g, na: (layer, g[j], 0, 0, 0)),
            pl.BlockSpec((None, None, epg, f, d), lambda j, g, na: (layer, g[j], 0, 0, 0)),
        ],
        out_specs=pl.BlockSpec((expert_tile, d), lambda j, g, na: (j, 0)),
    )
    return pl.pallas_call(
        _experts_kernel,
        out_shape=jax.ShapeDtypeStruct((n_rows, d), F32),
        grid_spec=grid_spec,
        compiler_params=_cparams(1),
        name="moe_experts",
    )(tile_grp, n_active, rows, p["moe_w_gate"], p["moe_w_up"], p["moe_w_down"])


def _experts_precise_kernel(grp_ref, na_ref, rows_ref, wg_ref, wu_ref, wd_ref, o_ref):
    j = pl.program_id(0)
    e = pl.program_id(1)

    @pl.when(e == 0)
    def _():
        o_ref[...] = jnp.zeros_like(o_ref)

    @pl.when(j < na_ref[0])
    def _():
        d = o_ref.shape[1]
        h = rows_ref[:, :d]
        gate = sum(jnp.where(e == k, rows_ref[:, d + k:d + k + 1], 0.0) for k in range(EXPERTS_PER_GROUP))
        a = _silu(_dot3(h, wg_ref[...])) * _dot3(h, wu_ref[...]) * gate
        o_ref[...] += _dot3(a, wd_ref[...])


def _experts_precise_call(rows, tile_grp, n_active, layer, p, expert_tile):
    n_rows, w = rows.shape
    _, _, epg, d, f = p["moe_w_gate"].shape

    def w_index(j, k, g, na):
        return (layer, g[j], jnp.where(j < na[0], k, epg - 1), 0, 0)

    grid_spec = pltpu.PrefetchScalarGridSpec(
        num_scalar_prefetch=2,
        grid=(n_rows // expert_tile, epg),
        in_specs=[
            pl.BlockSpec((expert_tile, w), lambda j, k, g, na: (jnp.minimum(j, na[0] - 1), 0)),
            pl.BlockSpec((None, None, None, d, f), w_index),
            pl.BlockSpec((None, None, None, d, f), w_index),
            pl.BlockSpec((None, None, None, f, d), w_index),
        ],
        out_specs=pl.BlockSpec((expert_tile, d), lambda j, k, g, na: (j, 0)),
    )
    return pl.pallas_call(
        _experts_precise_kernel,
        out_shape=jax.ShapeDtypeStruct((n_rows, d), F32),
        grid_spec=grid_spec,
        compiler_params=_cparams(2),
        name="moe_experts_precise",
    )(tile_grp, n_active, rows, p["moe_w_gate"], p["moe_w_up"], p["moe_w_down"])


def _row_plan(counts, gid, rank, expert_tile, n_tiles):
    nt = (counts + expert_tile - 1) // expert_tile
    cum = jnp.cumsum(nt)
    starts = (cum - nt) * expert_tile
    pos = starts[gid] + rank
    gap_lo = starts + counts
    gap_hi = jnp.concatenate([starts[1:], jnp.full((1,), n_tiles * expert_tile, jnp.int32)])
    n_active = cum[-1]
    j = jnp.minimum(jnp.arange(n_tiles, dtype=jnp.int32), n_active - 1)
    tile_grp = jnp.sum((j[:, None] >= cum[None, :]).astype(jnp.int32), axis=1)
    gaps = jnp.concatenate([gap_lo, gap_hi]).astype(jnp.int32)
    return pos.astype(jnp.int32), gaps, tile_grp.astype(jnp.int32), n_active.reshape(1).astype(jnp.int32)


def _moe_post_kernel(pos_ref, x_ref, gt_ref, ng_ref, nb_ref, f_ref, o_ref, buf_ref, sem, *, tile):
    i = pl.program_id(0)
    n = pl.num_programs(0)

    def issue(step, slot):
        base = step * tile

        def body(c, carry):
            for u in range(SUBLANES):
                src = _row_of(f_ref, pos_ref[base + c * SUBLANES + u])
                pltpu.make_async_copy(src, _row_of(buf_ref.at[slot], (c, u)), sem.at[slot]).start()
            return carry

        lax.fori_loop(0, tile // SUBLANES, body, 0)

    @pl.when(i == 0)
    def _():
        issue(0, 0)

    slot = i % 2

    @pl.when(i + 1 < n)
    def _():
        issue(i + 1, 1 - slot)

    pltpu.make_async_copy(f_ref.at[pl.ds(0, tile // SUBLANES)], buf_ref.at[slot], sem.at[slot]).wait()
    x = x_ref[...]
    f = buf_ref[slot].reshape(x.shape)
    o_ref[...] = _layer_norm(DEEPNORM_ALPHA * x + (1.0 + gt_ref[...]) * f, ng_ref[...], nb_ref[...])


def _moe_post_call(grp, x, layer, f_rows, pos, p):
    d = x.shape[1]
    grid_spec = pltpu.PrefetchScalarGridSpec(
        num_scalar_prefetch=1,
        grid=(grp.n_tiles,),
        in_specs=[grp.row_spec(d), grp.mod_spec(layer, 5), _at((1, d), layer, 1), _at((1, d), layer, 1),
                  pl.BlockSpec(memory_space=pl.ANY)],
        out_specs=grp.row_spec(d),
        scratch_shapes=[pltpu.VMEM((2, grp.tile // SUBLANES, SUBLANES, d), F32),
                        pltpu.SemaphoreType.DMA((2,))],
    )
    return pl.pallas_call(
        functools.partial(_moe_post_kernel, tile=grp.tile),
        out_shape=jax.ShapeDtypeStruct((grp.n, d), F32),
        grid_spec=grid_spec,
        compiler_params=_cparams(1),
        name="moe_post",
    )(pos, x, grp.mods, p["ln_g"], p["ln_b"], f_rows.reshape(f_rows.shape[0] // SUBLANES, SUBLANES, d))


def _moe_sublayer(grp, x, layer, p, expert_tile):
    if grp.precise:
        hr, gid, rank, cnt = _router_call(grp, x, layer, p)
        n_tiles = grp.n // expert_tile + N_EXPERT_GROUPS
        pos, gaps, tile_grp, n_active = _row_plan(cnt[:N_EXPERT_GROUPS, 0], gid.reshape(grp.n),
                                                  rank.reshape(grp.n), expert_tile, n_tiles)
        rows = _scatter_call(grp, hr, pos, gaps, n_tiles * expert_tile)
        f_rows = _experts_precise_call(rows, tile_grp, n_active, layer, p, expert_tile)
    else:
        srt, gid, lrank, tcnt = _router_sorted_call(grp, x, layer, p)
        max_rows = grp.n + grp.n_tiles * N_EXPERT_GROUPS * (SUBLANES - 1)
        n_tiles = -(-max_rows // expert_tile) + N_EXPERT_GROUPS
        pos, runs, gaps8, tile_grp, n_active = _run_plan(
            tcnt[:, :N_EXPERT_GROUPS, 0], gid.reshape(grp.n), lrank.reshape(grp.n), grp.tile,
            expert_tile, n_tiles)
        rows = _scatter_runs_call(grp, srt, runs, gaps8, n_tiles * expert_tile)
        f_rows = _experts_call(rows, tile_grp, n_active, layer, p, expert_tile)
    return _moe_post_call(grp, x, layer, f_rows, pos, p)


def _kv_kernel(x_ref, wk_ref, wv_ref, cos_ref, slo_ref, shi_ref, *out_refs, precise):
    x = x_ref[...]
    k = _rope(_mm(x, wk_ref, precise), cos_ref[...], slo_ref[...], shi_ref[...])
    v = _mm(x, wv_ref, precise)
    if len(out_refs) == 2:
        out_refs[0][...] = k
        out_refs[1][...] = v
    else:
        kt = k.T
        out_refs[0][...] = kt
        out_refs[1][...] = v
        out_refs[2][...] = kt.astype(BF16)
        out_refs[3][...] = v.astype(BF16)


def _rope_spec(grp, tables_rows):
    if tables_rows == 1:
        return _full_spec((1, LANES))
    tiles = tables_rows // grp.tile
    return pl.BlockSpec((grp.tile, LANES), lambda i, *_: (i % tiles, 0))


def _kv_call(grp, x, p, rope_tabs, seq):
    d = x.shape[1]
    wk_cols, wv_cols = p["kv_w_k"].shape[1], p["kv_w_v"].shape[1]
    rs = _rope_spec(grp, rope_tabs[0].shape[0])
    if seq is None:
        out_shape = [jax.ShapeDtypeStruct((grp.n, wk_cols), F32), jax.ShapeDtypeStruct((grp.n, wv_cols), F32)]
        out_specs = [grp.row_spec(wk_cols), grp.row_spec(wv_cols)]
    else:
        tiles = seq // grp.tile
        kt_spec = pl.BlockSpec((None, wk_cols, grp.tile), lambda i: (i // tiles, 0, i % tiles))
        out_shape = [jax.ShapeDtypeStruct((grp.n // seq, wk_cols, seq), F32),
                     jax.ShapeDtypeStruct((grp.n, wv_cols), F32),
                     jax.ShapeDtypeStruct((grp.n // seq, wk_cols, seq), BF16),
                     jax.ShapeDtypeStruct((grp.n, wv_cols), BF16)]
        out_specs = [kt_spec, grp.row_spec(wv_cols), kt_spec, grp.row_spec(wv_cols)]
    return pl.pallas_call(
        functools.partial(_kv_kernel, precise=grp.precise),
        out_shape=tuple(out_shape),
        grid=(grp.n_tiles,),
        in_specs=[grp.row_spec(d), _full_spec((d, wk_cols)), _full_spec((d, wv_cols)), rs, rs, rs],
        out_specs=tuple(out_specs),
        compiler_params=_cparams(1),
        name="kv_proj",
    )(x, p["kv_w_k"], p["kv_w_v"], *rope_tabs)


def _q_kernel(x_ref, sh_ref, sc_ref, wq_ref, cos_ref, slo_ref, shi_ref, o_ref, *, precise):
    h = x_ref[...] * (1.0 + sc_ref[...]) + sh_ref[...]
    q = _rope(_mm(h, wq_ref, precise), cos_ref[...], slo_ref[...], shi_ref[...])
    o_ref[...] = (q * (HEAD_DIM ** -0.5)).astype(o_ref.dtype)


def _q_call(grp, x, layer, p, rope_tabs, out_dtype):
    d = x.shape[1]
    j = layer - N_A_LAYERS
    cols = p["b_w_q"].shape[-1]
    rs = _rope_spec(grp, rope_tabs[0].shape[0])
    return pl.pallas_call(
        functools.partial(_q_kernel, precise=grp.precise),
        out_shape=jax.ShapeDtypeStruct((grp.n, cols), out_dtype),
        grid=(grp.n_tiles,),
        in_specs=[grp.row_spec(d), grp.mod_spec(layer, 0), grp.mod_spec(layer, 1),
                  _at((d, cols), j), rs, rs, rs],
        out_specs=grp.row_spec(cols),
        compiler_params=_cparams(1),
        name="q_proj",
    )(x, grp.mods, grp.mods, p["b_w_q"], *rope_tabs)


def _diff_lambda(lamv_ref, lam_init):
    lv = lamv_ref[...]
    a = jnp.exp(jnp.sum(lv[0:1] * lv[1:2], axis=1, keepdims=True))
    b = jnp.exp(jnp.sum(lv[2:3] * lv[3:4], axis=1, keepdims=True))
    return a - b + lam_init


def _attn_prompt_kernel(q_ref, kt_ref, v_ref, lamv_ref, sg_ref, o_ref, s_ref, p_ref, v1_ref,
                        *, lam_init, t):
    seq = q_ref.shape[0]
    rows_per_chunk = SOFTMAX_ROWS
    v1_ref[:, :V_DIM] = v_ref[...]
    v1_ref[:, V_DIM:] = jnp.ones((seq, V_DIM), BF16)
    lam = _diff_lambda(lamv_ref, lam_init)
    out_scale = sg_ref[...] * (1.0 - lam_init)
    lane = lax.broadcasted_iota(jnp.int32, (t, V_DIM), 1)
    row_i = lax.broadcasted_iota(jnp.int32, (rows_per_chunk, LANES), 0)
    col_i = lax.broadcasted_iota(jnp.int32, (rows_per_chunk, LANES), 1)
    for qi in range(seq // t):
        slot = qi % 2
        kend = (qi + 1) * t
        q = q_ref[qi * t:(qi + 1) * t, :]
        zero = jnp.zeros_like(q)
        qq = jnp.concatenate([jnp.where(lane < HEAD_DIM, q, zero), jnp.where(lane >= HEAD_DIM, q, zero)], axis=0)
        s_ref[slot, :, :kend] = _dot(qq, kt_ref[:, :kend])
        for rc in range(2 * t // rows_per_chunk):
            r0 = rc * rows_per_chunk
            q0 = r0 % t
            rows = slice(r0, r0 + rows_per_chunk)

            def load(cb):
                sc = s_ref[slot, rows, cb * LANES:(cb + 1) * LANES]
                c0 = cb * LANES - qi * t
                if c0 + LANES - 1 > q0:
                    sc = jnp.where(col_i + c0 <= row_i + q0, sc, NEG_BIG)
                return sc

            n_cb = min(kend, qi * t + q0 + rows_per_chunk + LANES - 1) // LANES
            n_cb = min(n_cb, kend // LANES)
            m = load(0)
            for cb in range(1, n_cb):
                m = jnp.maximum(m, load(cb))
            m = jnp.broadcast_to(jnp.max(m, axis=1, keepdims=True), m.shape)
            for cb in range(kend // LANES):
                cols = slice(cb * LANES, (cb + 1) * LANES)
                if cb < n_cb:
                    p_ref[slot, rows, cols] = jnp.exp(load(cb) - m).astype(BF16)
                else:
                    p_ref[slot, rows, cols] = jnp.zeros((rows_per_chunk, LANES), BF16)
        acc = _dot(p_ref[slot, :, :kend], v1_ref[:kend, :])
        o = acc[:, :V_DIM] / acc[:, V_DIM:]
        od = o[:t] - lam * o[t:]
        ms = jnp.mean(od * od, axis=-1, keepdims=True)
        o_ref[qi * t:(qi + 1) * t, :] = (od * lax.rsqrt(ms + LN_EPS) * out_scale).astype(o_ref.dtype)


def _attn_prompt_call(q, kt, v, layer, p, lam_init, batch, seq, t):
    n, cols = q.shape
    j = layer - N_A_LAYERS
    per_head = pl.BlockSpec((seq, V_DIM), lambda b, h: (b, h))
    return pl.pallas_call(
        functools.partial(_attn_prompt_kernel, lam_init=lam_init, t=t),
        out_shape=jax.ShapeDtypeStruct((n, cols), BF16),
        grid=(batch, N_HEADS),
        in_specs=[per_head, pl.BlockSpec((None, V_DIM, seq), lambda b, h: (b, h, 0)), per_head,
                  _at((4, HEAD_DIM), j), _at((1, V_DIM), j)],
        out_specs=per_head,
        scratch_shapes=[pltpu.VMEM((2, 2 * t, seq), F32), pltpu.VMEM((2, 2 * t, seq), BF16),
                        pltpu.VMEM((seq, 2 * V_DIM), BF16)],
        compiler_params=_cparams(2),
        name="attn_prompt",
    )(q, kt, v, p["lamv"], p["subln_g"])


def _attn_sample_kernel(pt_ref, q_ref, kn_ref, vn_ref, lamv_ref, sg_ref, *rest, lam_init):
    del pt_ref
    npg = PAGES_PER_STEP
    k_refs, v_refs = rest[:npg], rest[npg:2 * npg]
    o_ref, qcol_ref, m_ref, l_ref, acc_ref = rest[2 * npg:]
    step = pl.program_id(1)
    n_rows = 2 * N_HEADS
    cols = q_ref.shape[1]

    def seg_sums(prod):
        return jnp.sum(prod.reshape(n_rows, HEAD_DIM, prod.shape[1]), axis=1)

    def as_column(row_vec):
        return jnp.transpose(jnp.broadcast_to(row_vec, (LANES, cols)))

    @pl.when(step == 0)
    def _():
        qcol = as_column(q_ref[...])
        qcol_ref[...] = qcol
        m_ref[...] = seg_sums(qcol * as_column(kn_ref[...]))[:, :1]
        l_ref[...] = jnp.ones_like(l_ref)
        vn = vn_ref[...]
        acc_ref[...] = jnp.concatenate(
            [jnp.broadcast_to(vn[:, h * V_DIM:(h + 1) * V_DIM], (2, V_DIM)) for h in range(N_HEADS)], axis=0)

    def segment_scores(r):
        f0 = r * HEAD_DIM
        qseg = qcol_ref[f0:f0 + HEAD_DIM, :]
        return jnp.concatenate(
            [jnp.sum(k_ref[f0:f0 + HEAD_DIM, :] * qseg, axis=0, keepdims=True) for k_ref in k_refs], axis=1)

    s = jnp.concatenate([segment_scores(r) for r in range(n_rows)], axis=0)
    m_old = m_ref[...]
    m_new = jnp.maximum(m_old, jnp.max(s, axis=1, keepdims=True))
    a = jnp.exp(m_old - m_new)
    p = jnp.exp(s - m_new)
    l_ref[...] = a * l_ref[...] + jnp.sum(p, axis=1, keepdims=True)
    m_ref[...] = m_new
    p_hi, p_lo = _split_bf16(p)
    prow = lax.broadcasted_iota(jnp.int32, p.shape, 0)
    zero = jnp.zeros((), BF16)
    pv = jnp.zeros(acc_ref.shape, F32)
    for h in range(N_HEADS):
        vh = jnp.concatenate([r[pl.ds(h, PAGE_SIZE, stride=N_HEADS), :] for r in v_refs], axis=0)
        v_hi, v_lo = _split_bf16(vh)
        mine = (prow // 2) == h
        ph_hi = jnp.where(mine, p_hi, zero)
        r = _dot(jnp.concatenate([ph_hi, jnp.where(mine, p_lo, zero)], axis=0), v_hi)
        pv = pv + r[:n_rows] + r[n_rows:] + _dot(ph_hi, v_lo)
    acc_ref[...] = a * acc_ref[...] + pv

    @pl.when(step == pl.num_programs(1) - 1)
    def _():
        o = acc_ref[...] / l_ref[...]
        lam = _diff_lambda(lamv_ref, lam_init)
        outs = []
        for h in range(N_HEADS):
            od = o[2 * h:2 * h + 1] - lam * o[2 * h + 1:2 * h + 2]
            ms = jnp.mean(od * od, axis=-1, keepdims=True)
            outs.append(od * lax.rsqrt(ms + LN_EPS) * sg_ref[...] * (1.0 - lam_init))
        o_ref[...] = jnp.concatenate(outs, axis=1)


def _attn_sample_call(q, k_new, v_new, cache_kt, cache_v2, page_table, layer, p, lam_init):
    n_seq, cols = q.shape
    n_pages = page_table.shape[1]
    npg = PAGES_PER_STEP
    j = layer - N_A_LAYERS
    row3 = lambda a: a.reshape(n_seq, 1, cols)
    row_spec = pl.BlockSpec((None, 1, cols), lambda b, g, pt: (b, 0, 0))

    def page_spec(i, shape):
        return pl.BlockSpec((None,) + shape, lambda b, g, pt: (pt[b * n_pages + g * npg + i], 0, 0))

    grid_spec = pltpu.PrefetchScalarGridSpec(
        num_scalar_prefetch=1,
        grid=(n_seq, n_pages // npg),
        in_specs=[row_spec, row_spec, row_spec, _at((4, HEAD_DIM), j), _at((1, V_DIM), j)]
        + [page_spec(i, cache_kt.shape[1:]) for i in range(npg)]
        + [page_spec(i, cache_v2.shape[1:]) for i in range(npg)],
        out_specs=row_spec,
        scratch_shapes=[pltpu.VMEM((cols, LANES), F32), pltpu.VMEM((2 * N_HEADS, 1), F32),
                        pltpu.VMEM((2 * N_HEADS, 1), F32), pltpu.VMEM((2 * N_HEADS, V_DIM), F32)],
    )
    out = pl.pallas_call(
        functools.partial(_attn_sample_kernel, lam_init=lam_init),
        out_shape=jax.ShapeDtypeStruct((n_seq, 1, cols), F32),
        grid_spec=grid_spec,
        compiler_params=_cparams(2),
        name="attn_sample",
    )(page_table.reshape(-1), row3(q), row3(k_new), row3(v_new), p["lamv"], p["subln_g"],
      *([cache_kt] * npg), *([cache_v2] * npg))
    return out.reshape(n_seq, cols)


def _attn_out_kernel(o_ref, x_ref, gt_ref, wo_ref, ng_ref, nb_ref, y_ref, *, precise):
    y = _mm(o_ref[...], wo_ref, precise)
    x = x_ref[...]
    y_ref[...] = _layer_norm(DEEPNORM_ALPHA * x + (1.0 + gt_ref[...]) * y, ng_ref[...], nb_ref[...])


def _attn_out_call(grp, o, x, layer, p):
    d = x.shape[1]
    cols = o.shape[1]
    j = layer - N_A_LAYERS
    return pl.pallas_call(
        functools.partial(_attn_out_kernel, precise=grp.precise),
        out_shape=jax.ShapeDtypeStruct((grp.n, d), F32),
        grid=(grp.n_tiles,),
        in_specs=[grp.row_spec(cols), grp.row_spec(d), grp.mod_spec(layer, 2), _at((cols, d), j),
                  _at((1, d), layer, 0), _at((1, d), layer, 0)],
        out_specs=grp.row_spec(d),
        compiler_params=_cparams(1),
        name="attn_out",
    )(o, x, grp.mods, p["b_w_o"], p["ln_g"], p["ln_b"])


def _rope_tables(pos):
    half = ROT_DIM // 2
    inv = 1.0 / (ROPE_THETA ** (jnp.arange(0, ROT_DIM, 2, dtype=F32) / ROT_DIM))
    ang = pos.astype(F32)[:, None] * inv[None, :]
    cos, sin = jnp.cos(ang), jnp.sin(ang)
    n = pos.shape[0]
    ones = jnp.ones((n, HEAD_DIM - ROT_DIM), F32)
    zeros = jnp.zeros((n, HEAD_DIM - ROT_DIM), F32)
    zh = jnp.zeros((n, half), F32)
    cos_t = jnp.concatenate([cos, cos, ones], axis=1)
    sin_lo = jnp.concatenate([-sin, zh, zeros], axis=1)
    sin_hi = jnp.concatenate([zh, sin, zeros], axis=1)
    rep = LANES // HEAD_DIM
    return tuple(jnp.tile(t, (1, rep)) for t in (cos_t, sin_lo, sin_hi))


def _trunk(grp, x, p, rope_tabs, attend, expert_tile, seq):
    v_rows = []
    k_sh = v_sh = kv_attn = None
    for l in range(DEPTH):
        if l < N_A_LAYERS:
            x, v_a = _a_mixer_call(grp, x, l, p)
            v_rows.append(v_a)
        else:
            lam_init = 0.8 - 0.6 * math.exp(-0.3 * l)
            q = _q_call(grp, x, l, p, rope_tabs, F32 if seq is None else BF16)
            o = attend(q, kv_attn, l, p, lam_init)
            x = _attn_out_call(grp, o, x, l, p)
        x = _moe_sublayer(grp, x, l, p, expert_tile)
        if l == N_A_LAYERS - 1:
            outs = _kv_call(grp, x, p, rope_tabs, seq)
            k_sh, v_sh = outs[0], outs[1]
            kv_attn = (k_sh, v_sh) if seq is None else (outs[2], outs[3])
    return x, k_sh, v_sh, v_rows


def kernel(x_prompt, x_sample, cache_k, cache_v, page_table, c_prompt, c_sample, ada_w, ada_b, ln_g, ln_b, a_w_in, a_ln_g, a_ln_b, a_w_s, a_b_s, a_w_out, kv_w_k, kv_w_v, b_w_q, b_lam_q1, b_lam_k1, b_lam_q2, b_lam_k2, b_subln_g, b_w_o, router_w, router_bias, moe_w_gate, moe_w_up, moe_w_down):
    batch, seq, d = x_prompt.shape
    dec_batch, dec_seq, _ = x_sample.shape
    assert dec_seq == 1, "the sample group is one new token per sequence"
    n_pool = cache_k.shape[0]
    past = page_table.shape[1] * PAGE_SIZE
    depth, n_exp, _, d_exp = moe_w_gate.shape
    n_a, d_a = a_ln_g.shape
    n_b = b_w_q.shape[0]
    gd = d_a // A_GROUPS
    epg = EXPERTS_PER_GROUP

    p_f32 = dict(
        a_w_in=a_w_in, a_w_out=a_w_out, kv_w_k=kv_w_k, kv_w_v=kv_w_v, b_w_q=b_w_q, b_w_o=b_w_o,
        moe_w_gate=moe_w_gate.reshape(depth, n_exp // epg, epg, d, d_exp),
        moe_w_up=moe_w_up.reshape(depth, n_exp // epg, epg, d, d_exp),
        moe_w_down=moe_w_down.reshape(depth, n_exp // epg, epg, d_exp, d),
        a_ln_g=a_ln_g.reshape(n_a, 1, d_a), a_ln_b=a_ln_b.reshape(n_a, 1, d_a),
        a_w_s=a_w_s, a_b_s_t=jnp.swapaxes(a_b_s, 1, 2),
        a_coef=jnp.repeat(a_w_s[:, :, 0, 0], gd, axis=1).reshape(n_a, 1, d_a),
        a_bias=jnp.repeat(a_b_s[:, :, 0], gd, axis=1).reshape(n_a, 1, d_a),
        ln_g=ln_g.reshape(depth, 2, 1, d), ln_b=ln_b.reshape(depth, 2, 1, d),
        lamv=jnp.stack([b_lam_q1, b_lam_k1, b_lam_q2, b_lam_k2], axis=1),
        subln_g=b_subln_g.reshape(n_b, 1, V_DIM),
        router_wt=router_w.T, router_b=router_bias.reshape(N_EXPERTS, 1),
    )
    matmul_weights = ("a_w_in", "a_w_out", "kv_w_k", "kv_w_v", "b_w_q", "b_w_o",
                      "moe_w_gate", "moe_w_up", "moe_w_down")
    p_bf16 = {k: (v.astype(BF16) if k in matmul_weights else v) for k, v in p_f32.items()}

    mods = _ada_call(jnp.concatenate([c_sample, c_prompt], axis=0), ada_w, ada_b)

    gp = _Group(batch * seq, PROMPT_TILE, seq, dec_batch, mods)
    tabs_p = _rope_tables(jnp.arange(seq, dtype=jnp.int32))

    def attend_prompt(q, kv, layer, p, lam_init):
        return _attn_prompt_call(q, kv[0], kv[1], layer, p, lam_init, batch, seq, ATTN_TILE)

    y_p, kt_p, v_p, _ = _trunk(gp, x_prompt.reshape(batch * seq, d), p_bf16, tabs_p, attend_prompt,
                               PROMPT_EXPERT_TILE, seq)
    k_p = jnp.transpose(kt_p.reshape(batch, N_HEADS, 2, HEAD_DIM, seq), (0, 4, 1, 2, 3))

    gs = _Group(dec_batch, dec_batch, 1, 0, mods)
    tabs_s = _rope_tables(jnp.full((1,), past, dtype=jnp.int32))
    cache_kt = jnp.transpose(cache_k, (0, 2, 3, 4, 1)).reshape(n_pool, N_HEADS * 2 * HEAD_DIM, PAGE_SIZE)
    cache_v2 = cache_v.reshape(n_pool, PAGE_SIZE * N_HEADS, V_DIM)

    def attend_sample(q, kv, layer, p, lam_init):
        return _attn_sample_call(q, kv[0], kv[1], cache_kt, cache_v2, page_table, layer, p, lam_init)

    y_s, k_s, v_s, v_rows = _trunk(gs, x_sample.reshape(dec_batch, d), p_f32, tabs_s, attend_sample,
                                   dec_batch, None)

    return (
        y_p.reshape(batch, seq, d),
        y_s.reshape(dec_batch, 1, d),
        k_p,
        v_p.reshape(batch, seq, N_HEADS, V_DIM),
        k_s.reshape(dec_batch, 1, N_HEADS, 2, HEAD_DIM),
        v_s.reshape(dec_batch, 1, N_HEADS, V_DIM),
        jnp.stack(v_rows, axis=0).reshape(N_A_LAYERS, dec_batch, 1, d_a),
    )
```

```python
import functools
import math

import jax
import jax.numpy as jnp
from jax import lax
from jax.experimental import pallas as pl
from jax.experimental.pallas import tpu as pltpu

DEPTH = 4
N_A_LAYERS = DEPTH // 2
CHUNK = 128
A_GROUPS = 8
N_HEADS = 8
HEAD_DIM = 64
V_DIM = 2 * HEAD_DIM
ROT_DIM = HEAD_DIM // 4
ROPE_THETA = 500000.0
N_EXPERTS = 16
N_EXPERT_GROUPS = 4
EXPERTS_PER_GROUP = N_EXPERTS // N_EXPERT_GROUPS
LN_EPS = 1e-5
DEEPNORM_ALPHA = (2 * DEPTH) ** 0.25
PAGE_SIZE = 128

LANES = 128
SUBLANES = 8
BF16_SUBLANES = 16
VMEM_LIMIT_BYTES = 56 * 1024 * 1024

GATE_LANES = LANES
NEG_BIG = -1e30
PAGES_PER_STEP = 16
SOFTMAX_ROWS = 32

PROMPT_TILE = 512
PROMPT_EXPERT_TILE = 512
ATTN_TILE = 256

F32 = jnp.float32
BF16 = jnp.bfloat16


def _cparams(n_axes):
    return pltpu.CompilerParams(
        dimension_semantics=("arbitrary",) * n_axes, vmem_limit_bytes=VMEM_LIMIT_BYTES)


def _dot(a, b):
    return jnp.dot(a, b, preferred_element_type=F32)


def _dot_nt(a, b):
    return lax.dot_general(a, b, (((1,), (1,)), ((), ())), preferred_element_type=F32)


def _split_bf16(a):
    hi = a.astype(BF16)
    return hi, (a - hi.astype(F32)).astype(BF16)


def _dot3(x, w):
    x_hi, x_lo = _split_bf16(x)
    w_hi, w_lo = _split_bf16(w)
    m = x.shape[0]
    if m % BF16_SUBLANES == 0:
        r = _dot(jnp.concatenate([x_hi, x_lo], axis=0), w_hi)
        return r[:m] + r[m:] + _dot(x_hi, w_lo)
    return _dot(x_hi, w_hi) + _dot(x_lo, w_hi) + _dot(x_hi, w_lo)


def _mm(x, w_ref, precise):
    if precise:
        return _dot3(x, w_ref[...])
    return _dot(x.astype(BF16), w_ref[...])


def _layer_norm(x, g, b):
    mu = jnp.mean(x, axis=-1, keepdims=True)
    xc = x - mu
    var = jnp.mean(xc * xc, axis=-1, keepdims=True)
    return xc * lax.rsqrt(var + LN_EPS) * g + b


def _silu(x):
    return x * (1.0 / (1.0 + jnp.exp(-x)))


def _gelu_tanh(x):
    return x * (0.5 * (1.0 + jnp.tanh(0.7978845608028654 * (x + 0.044715 * (x * x * x)))))


def _tile_lanes(t, reps):
    return jnp.concatenate([t] * reps, axis=1)


def _rope(x, cos_t, sin_lo, sin_hi):
    reps = x.shape[1] // LANES
    n = x.shape[1]
    nxt = pltpu.roll(x, n - ROT_DIM // 2, axis=1)
    prv = pltpu.roll(x, ROT_DIM // 2, axis=1)
    return (x * _tile_lanes(cos_t, reps) + nxt * _tile_lanes(sin_lo, reps)
            + prv * _tile_lanes(sin_hi, reps))


def _at(block, *lead):
    nd = len(block)
    return pl.BlockSpec((None,) * len(lead) + tuple(block), lambda *_: tuple(lead) + (0,) * nd)


def _full_spec(shape):
    return _at(shape)


def _ada_kernel(c_ref, w_ref, b_ref, o_ref):
    o_ref[...] = _dot3(_silu(c_ref[...]), w_ref[...]) + b_ref[...]


def _ada_call(c_all, ada_w, ada_b):
    depth, d, d6 = ada_w.shape
    r = c_all.shape[0]
    n_mod = d6 // d
    return pl.pallas_call(
        _ada_kernel,
        out_shape=jax.ShapeDtypeStruct((depth, n_mod, r, d), F32),
        grid=(depth, n_mod),
        in_specs=[
            pl.BlockSpec((r, d), lambda l, j: (0, 0)),
            pl.BlockSpec((None, d, d), lambda l, j: (l, 0, j)),
            pl.BlockSpec((None, None, 1, d), lambda l, j: (l, j, 0, 0)),
        ],
        out_specs=pl.BlockSpec((None, None, r, d), lambda l, j: (l, j, 0, 0)),
        compiler_params=_cparams(2),
        name="ada_mod",
    )(c_all, ada_w, ada_b.reshape(depth, n_mod, 1, d))


class _Group:
    def __init__(self, n_rows, tile, rows_per_cond, cond_row0, mods):
        self.n = n_rows
        self.tile = tile
        self.n_tiles = n_rows // tile
        self.per_row = rows_per_cond == 1
        self.precise = self.per_row
        self.mods = mods if self.per_row else mods.reshape(mods.shape[:3] + (1, mods.shape[3]))
        self.tiles_per_cond = None if self.per_row else rows_per_cond // tile
        self.cond_row0 = cond_row0
        self.d = mods.shape[-1]

    def mod_spec(self, layer, j):
        if self.per_row:
            return pl.BlockSpec((None, None, self.tile, self.d), lambda i, *_: (layer, j, 0, 0))
        tpc, r0 = self.tiles_per_cond, self.cond_row0
        return pl.BlockSpec((None, None, None, 1, self.d),
                            lambda i, *_: (layer, j, r0 + i // tpc, 0, 0))

    def row_spec(self, width):
        return pl.BlockSpec((self.tile, width), lambda i, *_: (i, 0))


def _a_mixer_prompt_kernel(x_ref, sh_ref, sc_ref, gt_ref, win_ref, lng_ref, lnb_ref, ws_ref,
                           bst_ref, wout_ref, ng_ref, nb_ref, o_ref):
    x = x_ref[...]
    t = x.shape[0]
    d_a = lng_ref.shape[1]
    h = (x * (1.0 + sc_ref[...]) + sh_ref[...]).astype(BF16)
    z = _gelu_tanh(_dot(h, win_ref[...]))
    u = z[:, :d_a]
    v = _layer_norm(z[:, d_a:], lng_ref[...], lnb_ref[...]).astype(BF16)
    causal = (lax.broadcasted_iota(jnp.int32, (CHUNK, CHUNK), 0)
              >= lax.broadcasted_iota(jnp.int32, (CHUNK, CHUNK), 1))
    gd = d_a // A_GROUPS
    cols = []
    for g in range(A_GROUPS):
        wsg = jnp.where(causal, ws_ref[g], 0.0).astype(BF16)
        bias = bst_ref[:, g:g + 1]
        rows = [_dot(wsg, v[c * CHUNK:(c + 1) * CHUNK, g * gd:(g + 1) * gd]) + bias
                for c in range(t // CHUNK)]
        cols.append(jnp.concatenate(rows, axis=0))
    s = jnp.concatenate(cols, axis=1)
    y = _dot((u * s).astype(BF16), wout_ref[...])
    o_ref[...] = _layer_norm(DEEPNORM_ALPHA * x + (1.0 + gt_ref[...]) * y, ng_ref[...], nb_ref[...])


def _a_mixer_sample_kernel(x_ref, sh_ref, sc_ref, gt_ref, win_ref, lng_ref, lnb_ref, coef_ref,
                           bias_ref, wout_ref, ng_ref, nb_ref, o_ref, v_ref):
    x = x_ref[...]
    d_a = lng_ref.shape[1]
    h = x * (1.0 + sc_ref[...]) + sh_ref[...]
    z = _gelu_tanh(_dot3(h, win_ref[...]))
    u = z[:, :d_a]
    v = _layer_norm(z[:, d_a:], lng_ref[...], lnb_ref[...])
    v_ref[...] = v
    s = v * coef_ref[...] + bias_ref[...]
    y = _dot3(u * s, wout_ref[...])
    o_ref[...] = _layer_norm(DEEPNORM_ALPHA * x + (1.0 + gt_ref[...]) * y, ng_ref[...], nb_ref[...])


def _a_mixer_call(grp, x, layer, p):
    d = x.shape[1]
    d_a = p["a_ln_g"].shape[-1]
    common_in = [
        grp.row_spec(d), grp.mod_spec(layer, 0), grp.mod_spec(layer, 1), grp.mod_spec(layer, 2),
        _at((d, 2 * d_a), layer), _at((1, d_a), layer), _at((1, d_a), layer),
    ]
    tail_in = [_at((d_a, d), layer), _at((1, d), layer, 0), _at((1, d), layer, 0)]
    args_head = (x, grp.mods, grp.mods, grp.mods, p["a_w_in"], p["a_ln_g"], p["a_ln_b"])
    args_tail = (p["a_w_out"], p["ln_g"], p["ln_b"])
    if grp.per_row:
        return pl.pallas_call(
            _a_mixer_sample_kernel,
            out_shape=(jax.ShapeDtypeStruct((grp.n, d), F32), jax.ShapeDtypeStruct((grp.n, d_a), F32)),
            grid=(grp.n_tiles,),
            in_specs=common_in + [_at((1, d_a), layer), _at((1, d_a), layer)] + tail_in,
            out_specs=(grp.row_spec(d), grp.row_spec(d_a)),
            compiler_params=_cparams(1),
            name="a_mixer_sample",
        )(*args_head, p["a_coef"], p["a_bias"], *args_tail)
    out = pl.pallas_call(
        _a_mixer_prompt_kernel,
        out_shape=jax.ShapeDtypeStruct((grp.n, d), F32),
        grid=(grp.n_tiles,),
        in_specs=common_in + [_at((A_GROUPS, CHUNK, CHUNK), layer), _at((CHUNK, A_GROUPS), layer)] + tail_in,
        out_specs=grp.row_spec(d),
        compiler_params=_cparams(1),
        name="a_mixer_prompt",
    )(*args_head, p["a_w_s"], p["a_b_s_t"], *args_tail)
    return out, None


def _top2_of4(a, b, c, d):
    hi1, lo1 = jnp.maximum(a, b), jnp.minimum(a, b)
    hi2, lo2 = jnp.maximum(c, d), jnp.minimum(c, d)
    return jnp.maximum(hi1, hi2) + jnp.maximum(jnp.minimum(hi1, hi2), jnp.maximum(lo1, lo2))


def _first_argmax4(vals):
    m = jnp.maximum(jnp.maximum(vals[0], vals[1]), jnp.maximum(vals[2], vals[3]))
    idx = jnp.where(vals[0] == m, 0, jnp.where(vals[1] == m, 1, jnp.where(vals[2] == m, 2, 3)))
    return m, idx


def _route(h, rwt_ref, rb_ref):
    t = h.shape[0]
    logits = lax.dot_general(rwt_ref[...], h, (((1,), (1,)), ((), ())),
                             precision=lax.Precision.HIGHEST, preferred_element_type=F32)
    mx = jnp.max(logits, axis=0, keepdims=True)
    ex = jnp.exp(logits - mx)
    scores = ex / jnp.sum(ex, axis=0, keepdims=True)
    sel = scores + rb_ref[...]
    sel_r = [sel[e:e + 1, :] for e in range(N_EXPERTS)]
    sc_r = [scores[e:e + 1, :] for e in range(N_EXPERTS)]
    epg = EXPERTS_PER_GROUP
    gs = [_top2_of4(*sel_r[g * epg:(g + 1) * epg]) for g in range(N_EXPERT_GROUPS)]
    best, gid = gs[0], jnp.zeros((1, t), jnp.int32)
    for g in range(1, N_EXPERT_GROUPS):
        upd = gs[g] > best
        gid = jnp.where(upd, g, gid)
        best = jnp.where(upd, gs[g], best)
    cand_sel, cand_sc = [], []
    for k in range(epg):
        a, b = sel_r[k], sc_r[k]
        for g in range(1, N_EXPERT_GROUPS):
            a = jnp.where(gid == g, sel_r[g * epg + k], a)
            b = jnp.where(gid == g, sc_r[g * epg + k], b)
        cand_sel.append(a)
        cand_sc.append(b)
    _, i1 = _first_argmax4(cand_sel)
    rest = [jnp.where(i1 == k, -jnp.inf, cand_sel[k]) for k in range(epg)]
    _, i2 = _first_argmax4(rest)
    w1 = sum(jnp.where(i1 == k, cand_sc[k], 0.0) for k in range(epg))
    w2 = sum(jnp.where(i2 == k, cand_sc[k], 0.0) for k in range(epg))
    tot = w1 + w2
    w1, w2 = w1 / tot, w2 / tot
    gate_rows = [jnp.where(i1 == k, w1, 0.0) + jnp.where(i2 == k, w2, 0.0) for k in range(epg)]
    gates_t = jnp.concatenate(gate_rows + [jnp.zeros((GATE_LANES - epg, t), F32)], axis=0)
    onehot = jnp.concatenate(
        [(gid == g).astype(F32) for g in range(N_EXPERT_GROUPS)]
        + [jnp.zeros((SUBLANES - N_EXPERT_GROUPS, t), F32)], axis=0)
    before = (lax.broadcasted_iota(jnp.int32, (t, t), 0)
              < lax.broadcasted_iota(jnp.int32, (t, t), 1)).astype(BF16)
    rank_in_tile = _dot(onehot.astype(BF16), before)
    return gid, gates_t.T, onehot, rank_in_tile


def _router_kernel(x_ref, sh_ref, sc_ref, rwt_ref, rb_ref, hr_ref, gid_ref, rank_ref, cnt_ref,
                   carry_ref):
    i = pl.program_id(0)
    d = x_ref.shape[1]

    @pl.when(i == 0)
    def _():
        carry_ref[...] = jnp.zeros_like(carry_ref)

    h = x_ref[...] * (1.0 + sc_ref[...]) + sh_ref[...]
    gid, gates, onehot, rank_in_tile = _route(h, rwt_ref, rb_ref)
    hr_ref[:, :d] = h
    hr_ref[:, d:] = gates
    carry = carry_ref[...]
    rank = jnp.sum(onehot * (rank_in_tile + carry[:, :1]), axis=0, keepdims=True)
    gid_ref[...] = gid
    rank_ref[...] = rank.astype(jnp.int32)
    carry = carry + jnp.sum(onehot, axis=1, keepdims=True)
    carry_ref[...] = carry
    cnt_ref[...] = carry.astype(jnp.int32)


def _router_sorted_kernel(x_ref, sh_ref, sc_ref, rwt_ref, rb_ref, rows_ref, gid_ref, rank_ref, cnt_ref):
    d = x_ref.shape[1]
    ts = rows_ref.shape[0]
    h = x_ref[...] * (1.0 + sc_ref[...]) + sh_ref[...]
    gid, gates, onehot, rank_in_tile = _route(h, rwt_ref, rb_ref)
    t = h.shape[0]
    cnt = jnp.sum(onehot, axis=1, keepdims=True)
    cnt8 = jnp.floor((cnt + (SUBLANES - 1)) * (1.0 / SUBLANES)) * SUBLANES
    lrank = jnp.sum(onehot * rank_in_tile, axis=0, keepdims=True)
    dest, off = lrank, jnp.zeros((1, 1), F32)
    for g in range(1, N_EXPERT_GROUPS):
        off = off + cnt8[g - 1:g, :]
        dest = dest + onehot[g:g + 1, :] * off
    place = (lax.broadcasted_iota(jnp.int32, (ts, t), 0) == dest.astype(jnp.int32)).astype(BF16)
    g_hi = gates.astype(BF16)
    g_r1 = gates - g_hi.astype(F32)
    g_mid = g_r1.astype(BF16)
    g_lo = (g_r1 - g_mid.astype(F32)).astype(BF16)
    placed = _dot(place, jnp.concatenate([h.astype(BF16), g_hi, g_mid, g_lo], axis=1))
    gl = GATE_LANES
    rows_ref[:, :d] = placed[:, :d]
    rows_ref[:, d:] = placed[:, d:d + gl] + placed[:, d + gl:d + 2 * gl] + placed[:, d + 2 * gl:]
    gid_ref[...] = gid
    rank_ref[...] = lrank.astype(jnp.int32)
    cnt_ref[...] = jnp.broadcast_to(cnt, cnt_ref.shape).astype(jnp.int32)


def _router_call(grp, x, layer, p):
    d = x.shape[1]
    per_token = pl.BlockSpec((1, grp.tile), lambda i: (0, i))
    return pl.pallas_call(
        _router_kernel,
        out_shape=(jax.ShapeDtypeStruct((grp.n, d + GATE_LANES), F32),
                   jax.ShapeDtypeStruct((1, grp.n), jnp.int32),
                   jax.ShapeDtypeStruct((1, grp.n), jnp.int32),
                   jax.ShapeDtypeStruct((SUBLANES, LANES), jnp.int32)),
        grid=(grp.n_tiles,),
        in_specs=[grp.row_spec(d), grp.mod_spec(layer, 3), grp.mod_spec(layer, 4),
                  _full_spec((N_EXPERTS, d)), _full_spec((N_EXPERTS, 1))],
        out_specs=(grp.row_spec(d + GATE_LANES), per_token, per_token, _full_spec((SUBLANES, LANES))),
        scratch_shapes=[pltpu.VMEM((SUBLANES, LANES), F32)],
        compiler_params=_cparams(1),
        name="moe_router",
    )(x, grp.mods, grp.mods, p["router_wt"], p["router_b"])


def _sorted_tile_rows(tile):
    return tile + N_EXPERT_GROUPS * SUBLANES


def _router_sorted_call(grp, x, layer, p):
    d = x.shape[1]
    ts = _sorted_tile_rows(grp.tile)
    per_token = pl.BlockSpec((1, grp.tile), lambda i: (0, i))
    return pl.pallas_call(
        _router_sorted_kernel,
        out_shape=(jax.ShapeDtypeStruct((grp.n_tiles * ts, d + GATE_LANES), F32),
                   jax.ShapeDtypeStruct((1, grp.n), jnp.int32),
                   jax.ShapeDtypeStruct((1, grp.n), jnp.int32),
                   jax.ShapeDtypeStruct((grp.n_tiles, SUBLANES, LANES), jnp.int32)),
        grid=(grp.n_tiles,),
        in_specs=[grp.row_spec(d), grp.mod_spec(layer, 3), grp.mod_spec(layer, 4),
                  _full_spec((N_EXPERTS, d)), _full_spec((N_EXPERTS, 1))],
        out_specs=(pl.BlockSpec((ts, d + GATE_LANES), lambda i: (i, 0)), per_token, per_token,
                   pl.BlockSpec((None, SUBLANES, LANES), lambda i: (i, 0, 0))),
        compiler_params=_cparams(1),
        name="moe_router_sorted",
    )(x, grp.mods, grp.mods, p["router_wt"], p["router_b"])


def _row_of(ref, row):
    if isinstance(row, tuple):
        return ref.at[row[0], pl.ds(row[1], 1), :]
    return ref.at[lax.shift_right_logical(row, 3), pl.ds(row & (SUBLANES - 1), 1), :]


def _scatter_kernel(pos_ref, gap_ref, hr_ref, out_ref, zero_ref, sem, zsem, *, tile):
    i = pl.program_id(0)
    base = i * tile

    def issue(c, carry):
        for u in range(SUBLANES):
            dst = _row_of(out_ref, pos_ref[base + c * SUBLANES + u])
            pltpu.make_async_copy(_row_of(hr_ref, (c, u)), dst, sem).start()
        return carry

    lax.fori_loop(0, tile // SUBLANES, issue, 0)

    @pl.when(i == pl.num_programs(0) - 1)
    def _():
        zero_ref[...] = jnp.zeros_like(zero_ref)
        for g in range(N_EXPERT_GROUPS):
            lo, hi = gap_ref[g], gap_ref[N_EXPERT_GROUPS + g]

            def zero_copy(r):
                return pltpu.make_async_copy(zero_ref.at[pl.ds(0, 1), :], _row_of(out_ref, r), zsem)

            def zissue(r, c):
                zero_copy(r).start()
                return c

            def zwait(r, c):
                zero_copy(r).wait()
                return c

            lax.fori_loop(lo, hi, zissue, 0)
            lax.fori_loop(lo, hi, zwait, 0)

    pltpu.make_async_copy(hr_ref, out_ref.at[pl.ds(0, tile // SUBLANES)], sem).wait()


def _scatter_call(grp, hr, pos, gaps, n_rows_out):
    w = hr.shape[1]
    grid_spec = pltpu.PrefetchScalarGridSpec(
        num_scalar_prefetch=2,
        grid=(grp.n_tiles,),
        in_specs=[pl.BlockSpec((grp.tile // SUBLANES, SUBLANES, w), lambda i, *_: (i, 0, 0))],
        out_specs=pl.BlockSpec(memory_space=pl.ANY),
        scratch_shapes=[pltpu.VMEM((SUBLANES, w), F32), pltpu.SemaphoreType.DMA(()),
                        pltpu.SemaphoreType.DMA(())],
    )
    out = pl.pallas_call(
        functools.partial(_scatter_kernel, tile=grp.tile),
        out_shape=jax.ShapeDtypeStruct((n_rows_out // SUBLANES, SUBLANES, w), F32),
        grid_spec=grid_spec,
        compiler_params=_cparams(1),
        name="moe_scatter",
    )(pos, gaps, hr.reshape(grp.n // SUBLANES, SUBLANES, w))
    return out.reshape(n_rows_out, w)


def _scatter_runs_kernel(run_ref, gap_ref, rows_ref, out_ref, zero_ref, sem, zsem, *, n_runs, max_len8):
    i = pl.program_id(0)
    n_bits = max_len8.bit_length()

    def run_copies(g):
        r = i * N_EXPERT_GROUPS + g
        src, dst, length = run_ref[r], run_ref[n_runs + r], run_ref[2 * n_runs + r]
        for b in range(n_bits):
            size = 1 << b
            off = lax.shift_left(lax.shift_right_logical(length, b + 1), b + 1)
            copy = pltpu.make_async_copy(rows_ref.at[pl.ds(src + off, size)],
                                         out_ref.at[pl.ds(dst + off, size)], sem)
            yield (lax.shift_right_logical(length, b) & 1) == 1, copy

    for g in range(N_EXPERT_GROUPS):
        for has_bit, copy in run_copies(g):
            pl.when(has_bit)(copy.start)

    @pl.when(i == pl.num_programs(0) - 1)
    def _():
        zero_ref[...] = jnp.zeros_like(zero_ref)
        for g in range(N_EXPERT_GROUPS):
            lo, hi = gap_ref[g], gap_ref[N_EXPERT_GROUPS + g]

            def zero_copy(r):
                return pltpu.make_async_copy(zero_ref, out_ref.at[pl.ds(r, 1)], zsem)

            def zissue(r, c):
                zero_copy(r).start()
                return c

            def zwait(r, c):
                zero_copy(r).wait()
                return c

            lax.fori_loop(lo, hi, zissue, 0)
            lax.fori_loop(lo, hi, zwait, 0)

    for g in range(N_EXPERT_GROUPS):
        for has_bit, copy in run_copies(g):
            pl.when(has_bit)(copy.wait)


def _scatter_runs_call(grp, rows, runs, gaps8, n_rows_out):
    w = rows.shape[1]
    ts8 = _sorted_tile_rows(grp.tile) // SUBLANES
    grid_spec = pltpu.PrefetchScalarGridSpec(
        num_scalar_prefetch=2,
        grid=(grp.n_tiles,),
        in_specs=[pl.BlockSpec((ts8, SUBLANES, w), lambda i, *_: (i, 0, 0))],
        out_specs=pl.BlockSpec(memory_space=pl.ANY),
        scratch_shapes=[pltpu.VMEM((1, SUBLANES, w), F32), pltpu.SemaphoreType.DMA(()),
                        pltpu.SemaphoreType.DMA(())],
    )
    out = pl.pallas_call(
        functools.partial(_scatter_runs_kernel, n_runs=grp.n_tiles * N_EXPERT_GROUPS, max_len8=ts8),
        out_shape=jax.ShapeDtypeStruct((n_rows_out // SUBLANES, SUBLANES, w), F32),
        grid_spec=grid_spec,
        compiler_params=_cparams(1),
        name="moe_scatter_runs",
    )(runs, gaps8, rows.reshape(rows.shape[0] // SUBLANES, SUBLANES, w))
    return out.reshape(n_rows_out, w)


def _run_plan(tile_counts, gid, lrank, tile, expert_tile, n_tiles):
    c8 = (tile_counts + SUBLANES - 1) // SUBLANES * SUBLANES
    run_src = jnp.cumsum(c8, axis=1) - c8
    rows_g = jnp.sum(c8, axis=0)
    nt = (rows_g + expert_tile - 1) // expert_tile
    cum = jnp.cumsum(nt)
    starts = (cum - nt) * expert_tile
    run_dst = starts[None, :] + jnp.cumsum(c8, axis=0) - c8
    gid_t = gid.reshape(-1, tile)
    pos = lrank.reshape(-1, tile)
    for g in range(N_EXPERT_GROUPS):
        pos = pos + jnp.where(gid_t == g, run_dst[:, g:g + 1], 0)
    pos = pos.reshape(-1)
    gap_lo = starts + rows_g
    gap_hi = jnp.concatenate([starts[1:], jnp.full((1,), n_tiles * expert_tile, jnp.int32)])
    n_active = cum[-1]
    j = jnp.minimum(jnp.arange(n_tiles, dtype=jnp.int32), n_active - 1)
    tile_grp = jnp.sum((j[:, None] >= cum[None, :]).astype(jnp.int32), axis=1)
    runs = (jnp.concatenate([run_src.reshape(-1), run_dst.reshape(-1), c8.reshape(-1)]) // SUBLANES)
    gaps8 = jnp.concatenate([gap_lo, gap_hi]) // SUBLANES
    return (pos.astype(jnp.int32), runs.astype(jnp.int32), gaps8.astype(jnp.int32),
            tile_grp.astype(jnp.int32), n_active.reshape(1).astype(jnp.int32))


def _experts_kernel(grp_ref, na_ref, rows_ref, wg_ref, wu_ref, wd_ref, o_ref):
    j = pl.program_id(0)

    @pl.when(j >= na_ref[0])
    def _():
        o_ref[...] = jnp.zeros_like(o_ref)

    @pl.when(j < na_ref[0])
    def _():
        d = o_ref.shape[1]
        hb = rows_ref[:, :d].astype(BF16)
        acts = []
        for e in range(EXPERTS_PER_GROUP):
            g = _dot(hb, wg_ref[e])
            u = _dot(hb, wu_ref[e])
            acts.append((_silu(g) * u * rows_ref[:, d + e:d + e + 1]).astype(BF16))
        a = jnp.concatenate(acts, axis=1)
        f = wd_ref.shape[1]
        o_ref[...] = _dot(a, wd_ref[...].reshape(EXPERTS_PER_GROUP * f, d))


def _experts_call(rows, tile_grp, n_active, layer, p, expert_tile):
    n_rows, w = rows.shape
    _, _, epg, d, f = p["moe_w_gate"].shape
    grid_spec = pltpu.PrefetchScalarGridSpec(
        num_scalar_prefetch=2,
        grid=(n_rows // expert_tile,),
        in_specs=[
            pl.BlockSpec((expert_tile, w), lambda j, g, na: (jnp.minimum(j, na[0] - 1), 0)),
            pl.BlockSpec((None, None, epg, d, f), lambda j, g, na: (layer, g[j], 0, 0, 0)),
            pl.BlockSpec((None, None, epg, d, f), lambda j, g, na: (layer, g[j], 0, 0, 0)),
            pl.BlockSpec((None, None, epg, f, d), lambda j, g, na: (layer, g[j], 0, 0, 0)),
        ],
        out_specs=pl.BlockSpec((expert_tile, d), lambda j, g, na: (j, 0)),
    )
    return pl.pallas_call(
        _experts_kernel,
        out_shape=jax.ShapeDtypeStruct((n_rows, d), F32),
        grid_spec=grid_spec,
        compiler_params=_cparams(1),
        name="moe_experts",
    )(tile_grp, n_active, rows, p["moe_w_gate"], p["moe_w_up"], p["moe_w_down"])


def _experts_precise_kernel(grp_ref, na_ref, rows_ref, wg_ref, wu_ref, wd_ref, o_ref):
    j = pl.program_id(0)
    e = pl.program_id(1)

    @pl.when(e == 0)
    def _():
        o_ref[...] = jnp.zeros_like(o_ref)

    @pl.when(j < na_ref[0])
    def _():
        d = o_ref.shape[1]
        h = rows_ref[:, :d]
        gate = sum(jnp.where(e == k, rows_ref[:, d + k:d + k + 1], 0.0) for k in range(EXPERTS_PER_GROUP))
        a = _silu(_dot3(h, wg_ref[...])) * _dot3(h, wu_ref[...]) * gate
        o_ref[...] += _dot3(a, wd_ref[...])


def _experts_precise_call(rows, tile_grp, n_active, layer, p, expert_tile):
    n_rows, w = rows.shape
    _, _, epg, d, f = p["moe_w_gate"].shape

    def w_index(j, k, g, na):
        return (layer, g[j], jnp.where(j < na[0], k, epg - 1), 0, 0)

    grid_spec = pltpu.PrefetchScalarGridSpec(
        num_scalar_prefetch=2,
        grid=(n_rows // expert_tile, epg),
        in_specs=[
            pl.BlockSpec((expert_tile, w), lambda j, k, g, na: (jnp.minimum(j, na[0] - 1), 0)),
            pl.BlockSpec((None, None, None, d, f), w_index),
            pl.BlockSpec((None, None, None, d, f), w_index),
            pl.BlockSpec((None, None, None, f, d), w_index),
        ],
        out_specs=pl.BlockSpec((expert_tile, d), lambda j, k, g, na: (j, 0)),
    )
    return pl.pallas_call(
        _experts_precise_kernel,
        out_shape=jax.ShapeDtypeStruct((n_rows, d), F32),
        grid_spec=grid_spec,
        compiler_params=_cparams(2),
        name="moe_experts_precise",
    )(tile_grp, n_active, rows, p["moe_w_gate"], p["moe_w_up"], p["moe_w_down"])


def _row_plan(counts, gid, rank, expert_tile, n_tiles):
    nt = (counts + expert_tile - 1) // expert_tile
    cum = jnp.cumsum(nt)
    starts = (cum - nt) * expert_tile
    pos = rank
    for g in range(N_EXPERT_GROUPS):
        pos = pos + jnp.where(gid == g, starts[g], 0)
    gap_lo = starts + counts
    gap_hi = jnp.concatenate([starts[1:], jnp.full((1,), n_tiles * expert_tile, jnp.int32)])
    n_active = cum[-1]
    j = jnp.minimum(jnp.arange(n_tiles, dtype=jnp.int32), n_active - 1)
    tile_grp = jnp.sum((j[:, None] >= cum[None, :]).astype(jnp.int32), axis=1)
    gaps = jnp.concatenate([gap_lo, gap_hi]).astype(jnp.int32)
    return pos.astype(jnp.int32), gaps, tile_grp.astype(jnp.int32), n_active.reshape(1).astype(jnp.int32)


def _moe_post_kernel(pos_ref, x_ref, gt_ref, ng_ref, nb_ref, f_ref, o_ref, buf_ref, sem, *, tile):
    i = pl.program_id(0)
    n = pl.num_programs(0)

    def issue(step, slot):
        base = step * tile

        def body(c, carry):
            for u in range(SUBLANES):
                src = _row_of(f_ref, pos_ref[base + c * SUBLANES + u])
                pltpu.make_async_copy(src, _row_of(buf_ref.at[slot], (c, u)), sem.at[slot]).start()
            return carry

        lax.fori_loop(0, tile // SUBLANES, body, 0)

    @pl.when(i == 0)
    def _():
        issue(0, 0)

    slot = i % 2

    @pl.when(i + 1 < n)
    def _():
        issue(i + 1, 1 - slot)

    pltpu.make_async_copy(f_ref.at[pl.ds(0, tile // SUBLANES)], buf_ref.at[slot], sem.at[slot]).wait()
    x = x_ref[...]
    f = buf_ref[slot].reshape(x.shape)
    o_ref[...] = _layer_norm(DEEPNORM_ALPHA * x + (1.0 + gt_ref[...]) * f, ng_ref[...], nb_ref[...])


def _moe_post_call(grp, x, layer, f_rows, pos, p):
    d = x.shape[1]
    grid_spec = pltpu.PrefetchScalarGridSpec(
        num_scalar_prefetch=1,
        grid=(grp.n_tiles,),
        in_specs=[grp.row_spec(d), grp.mod_spec(layer, 5), _at((1, d), layer, 1), _at((1, d), layer, 1),
                  pl.BlockSpec(memory_space=pl.ANY)],
        out_specs=grp.row_spec(d),
        scratch_shapes=[pltpu.VMEM((2, grp.tile // SUBLANES, SUBLANES, d), F32),
                        pltpu.SemaphoreType.DMA((2,))],
    )
    return pl.pallas_call(
        functools.partial(_moe_post_kernel, tile=grp.tile),
        out_shape=jax.ShapeDtypeStruct((grp.n, d), F32),
        grid_spec=grid_spec,
        compiler_params=_cparams(1),
        name="moe_post",
    )(pos, x, grp.mods, p["ln_g"], p["ln_b"], f_rows.reshape(f_rows.shape[0] // SUBLANES, SUBLANES, d))


def _moe_sublayer(grp, x, layer, p, expert_tile):
    if grp.precise:
        hr, gid, rank, cnt = _router_call(grp, x, layer, p)
        n_tiles = grp.n // expert_tile + N_EXPERT_GROUPS
        pos, gaps, tile_grp, n_active = _row_plan(cnt[:N_EXPERT_GROUPS, 0], gid.reshape(grp.n),
                                                  rank.reshape(grp.n), expert_tile, n_tiles)
        rows = _scatter_call(grp, hr, pos, gaps, n_tiles * expert_tile)
        f_rows = _experts_precise_call(rows, tile_grp, n_active, layer, p, expert_tile)
    else:
        srt, gid, lrank, tcnt = _router_sorted_call(grp, x, layer, p)
        max_rows = grp.n + grp.n_tiles * N_EXPERT_GROUPS * (SUBLANES - 1)
        n_tiles = -(-max_rows // expert_tile) + N_EXPERT_GROUPS
        pos, runs, gaps8, tile_grp, n_active = _run_plan(
            tcnt[:, :N_EXPERT_GROUPS, 0], gid.reshape(grp.n), lrank.reshape(grp.n), grp.tile,
            expert_tile, n_tiles)
        rows = _scatter_runs_call(grp, srt, runs, gaps8, n_tiles * expert_tile)
        f_rows = _experts_call(rows, tile_grp, n_active, layer, p, expert_tile)
    return _moe_post_call(grp, x, layer, f_rows, pos, p)


def _kv_kernel(x_ref, wk_ref, wv_ref, cos_ref, slo_ref, shi_ref, *out_refs, precise):
    x = x_ref[...]
    k = _rope(_mm(x, wk_ref, precise), cos_ref[...], slo_ref[...], shi_ref[...])
    v = _mm(x, wv_ref, precise)
    if len(out_refs) == 2:
        out_refs[0][...] = k
        out_refs[1][...] = v
    else:
        kt = k.T
        out_refs[0][...] = kt
        out_refs[1][...] = v
        out_refs[2][...] = kt.astype(BF16)
        out_refs[3][...] = v.astype(BF16)


def _rope_spec(grp, tables_rows):
    if tables_rows == 1:
        return _full_spec((1, LANES))
    tiles = tables_rows // grp.tile
    return pl.BlockSpec((grp.tile, LANES), lambda i, *_: (i % tiles, 0))


def _kv_call(grp, x, p, rope_tabs, seq):
    d = x.shape[1]
    wk_cols, wv_cols = p["kv_w_k"].shape[1], p["kv_w_v"].shape[1]
    rs = _rope_spec(grp, rope_tabs[0].shape[0])
    if seq is None:
        out_shape = [jax.ShapeDtypeStruct((grp.n, wk_cols), F32), jax.ShapeDtypeStruct((grp.n, wv_cols), F32)]
        out_specs = [grp.row_spec(wk_cols), grp.row_spec(wv_cols)]
    else:
        tiles = seq // grp.tile
        kt_spec = pl.BlockSpec((None, wk_cols, grp.tile), lambda i: (i // tiles, 0, i % tiles))
        out_shape = [jax.ShapeDtypeStruct((grp.n // seq, wk_cols, seq), F32),
                     jax.ShapeDtypeStruct((grp.n, wv_cols), F32),
                     jax.ShapeDtypeStruct((grp.n // seq, wk_cols, seq), BF16),
                     jax.ShapeDtypeStruct((grp.n, wv_cols), BF16)]
        out_specs = [kt_spec, grp.row_spec(wv_cols), kt_spec, grp.row_spec(wv_cols)]
    return pl.pallas_call(
        functools.partial(_kv_kernel, precise=grp.precise),
        out_shape=tuple(out_shape),
        grid=(grp.n_tiles,),
        in_specs=[grp.row_spec(d), _full_spec((d, wk_cols)), _full_spec((d, wv_cols)), rs, rs, rs],
        out_specs=tuple(out_specs),
        compiler_params=_cparams(1),
        name="kv_proj",
    )(x, p["kv_w_k"], p["kv_w_v"], *rope_tabs)


def _q_kernel(x_ref, sh_ref, sc_ref, wq_ref, cos_ref, slo_ref, shi_ref, o_ref, *, precise):
    h = x_ref[...] * (1.0 + sc_ref[...]) + sh_ref[...]
    q = _rope(_mm(h, wq_ref, precise), cos_ref[...], slo_ref[...], shi_ref[...])
    o_ref[...] = (q * (HEAD_DIM ** -0.5)).astype(o_ref.dtype)


def _q_call(grp, x, layer, p, rope_tabs, out_dtype):
    d = x.shape[1]
    j = layer - N_A_LAYERS
    cols = p["b_w_q"].shape[-1]
    rs = _rope_spec(grp, rope_tabs[0].shape[0])
    return pl.pallas_call(
        functools.partial(_q_kernel, precise=grp.precise),
        out_shape=jax.ShapeDtypeStruct((grp.n, cols), out_dtype),
        grid=(grp.n_tiles,),
        in_specs=[grp.row_spec(d), grp.mod_spec(layer, 0), grp.mod_spec(layer, 1),
                  _at((d, cols), j), rs, rs, rs],
        out_specs=grp.row_spec(cols),
        compiler_params=_cparams(1),
        name="q_proj",
    )(x, grp.mods, grp.mods, p["b_w_q"], *rope_tabs)


def _diff_lambda(lamv_ref, lam_init):
    lv = lamv_ref[...]
    a = jnp.exp(jnp.sum(lv[0:1] * lv[1:2], axis=1, keepdims=True))
    b = jnp.exp(jnp.sum(lv[2:3] * lv[3:4], axis=1, keepdims=True))
    return a - b + lam_init


def _attn_prompt_kernel(q_ref, kt_ref, v_ref, lamv_ref, sg_ref, o_ref, s_ref, p_ref, v1_ref,
                        *, lam_init, t):
    seq = q_ref.shape[0]
    rows_per_chunk = SOFTMAX_ROWS
    v1_ref[:, :V_DIM] = v_ref[...]
    v1_ref[:, V_DIM:] = jnp.ones((seq, V_DIM), BF16)
    lam = _diff_lambda(lamv_ref, lam_init)
    out_scale = sg_ref[...] * (1.0 - lam_init)
    lane = lax.broadcasted_iota(jnp.int32, (t, V_DIM), 1)
    row_i = lax.broadcasted_iota(jnp.int32, (rows_per_chunk, LANES), 0)
    col_i = lax.broadcasted_iota(jnp.int32, (rows_per_chunk, LANES), 1)
    for qi in range(seq // t):
        slot = qi % 2
        kend = (qi + 1) * t
        q = q_ref[qi * t:(qi + 1) * t, :]
        zero = jnp.zeros_like(q)
        qq = jnp.concatenate([jnp.where(lane < HEAD_DIM, q, zero), jnp.where(lane >= HEAD_DIM, q, zero)], axis=0)
        s_ref[slot, :, :kend] = _dot(qq, kt_ref[:, :kend])
        for rc in range(2 * t // rows_per_chunk):
            r0 = rc * rows_per_chunk
            q0 = r0 % t
            rows = slice(r0, r0 + rows_per_chunk)

            def load(cb):
                sc = s_ref[slot, rows, cb * LANES:(cb + 1) * LANES]
                c0 = cb * LANES - qi * t
                if c0 + LANES - 1 > q0:
                    sc = jnp.where(col_i + c0 <= row_i + q0, sc, NEG_BIG)
                return sc

            n_cb = min(kend, qi * t + q0 + rows_per_chunk + LANES - 1) // LANES
            n_cb = min(n_cb, kend // LANES)
            m = load(0)
            for cb in range(1, n_cb):
                m = jnp.maximum(m, load(cb))
            m = jnp.broadcast_to(jnp.max(m, axis=1, keepdims=True), m.shape)
            for cb in range(kend // LANES):
                cols = slice(cb * LANES, (cb + 1) * LANES)
                if cb < n_cb:
                    p_ref[slot, rows, cols] = jnp.exp(load(cb) - m).astype(BF16)
                else:
                    p_ref[slot, rows, cols] = jnp.zeros((rows_per_chunk, LANES), BF16)
        acc = _dot(p_ref[slot, :, :kend], v1_ref[:kend, :])
        o = acc[:, :V_DIM] / acc[:, V_DIM:]
        od = o[:t] - lam * o[t:]
        ms = jnp.mean(od * od, axis=-1, keepdims=True)
        o_ref[qi * t:(qi + 1) * t, :] = (od * lax.rsqrt(ms + LN_EPS) * out_scale).astype(o_ref.dtype)


def _attn_prompt_call(q, kt, v, layer, p, lam_init, batch, seq, t):
    n, cols = q.shape
    j = layer - N_A_LAYERS
    per_head = pl.BlockSpec((seq, V_DIM), lambda b, h: (b, h))
    return pl.pallas_call(
        functools.partial(_attn_prompt_kernel, lam_init=lam_init, t=t),
        out_shape=jax.ShapeDtypeStruct((n, cols), BF16),
        grid=(batch, N_HEADS),
        in_specs=[per_head, pl.BlockSpec((None, V_DIM, seq), lambda b, h: (b, h, 0)), per_head,
                  _at((4, HEAD_DIM), j), _at((1, V_DIM), j)],
        out_specs=per_head,
        scratch_shapes=[pltpu.VMEM((2, 2 * t, seq), F32), pltpu.VMEM((2, 2 * t, seq), BF16),
                        pltpu.VMEM((seq, 2 * V_DIM), BF16)],
        compiler_params=_cparams(2),
        name="attn_prompt",
    )(q, kt, v, p["lamv"], p["subln_g"])


def _attn_sample_kernel(pt_ref, q_ref, kn_ref, vn_ref, lamv_ref, sg_ref, *rest, lam_init):
    del pt_ref
    npg = PAGES_PER_STEP
    k_refs, v_refs = rest[:npg], rest[npg:2 * npg]
    o_ref, qcol_ref, m_ref, l_ref, acc_ref = rest[2 * npg:]
    step = pl.program_id(1)
    n_rows = 2 * N_HEADS
    cols = q_ref.shape[1]

    def seg_sums(prod):
        return jnp.sum(prod.reshape(n_rows, HEAD_DIM, prod.shape[1]), axis=1)

    def as_column(row_vec):
        return jnp.transpose(jnp.broadcast_to(row_vec, (LANES, cols)))

    @pl.when(step == 0)
    def _():
        qcol = as_column(q_ref[...])
        qcol_ref[...] = qcol
        m_ref[...] = seg_sums(qcol * as_column(kn_ref[...]))[:, :1]
        l_ref[...] = jnp.ones_like(l_ref)
        vn = vn_ref[...]
        acc_ref[...] = jnp.concatenate(
            [jnp.broadcast_to(vn[:, h * V_DIM:(h + 1) * V_DIM], (2, V_DIM)) for h in range(N_HEADS)], axis=0)

    def segment_scores(r):
        f0 = r * HEAD_DIM
        qseg = qcol_ref[f0:f0 + HEAD_DIM, :]
        return jnp.concatenate(
            [jnp.sum(k_ref[f0:f0 + HEAD_DIM, :] * qseg, axis=0, keepdims=True) for k_ref in k_refs], axis=1)

    s = jnp.concatenate([segment_scores(r) for r in range(n_rows)], axis=0)
    m_old = m_ref[...]
    m_new = jnp.maximum(m_old, jnp.max(s, axis=1, keepdims=True))
    a = jnp.exp(m_old - m_new)
    p = jnp.exp(s - m_new)
    l_ref[...] = a * l_ref[...] + jnp.sum(p, axis=1, keepdims=True)
    m_ref[...] = m_new
    p_hi, p_lo = _split_bf16(p)
    prow = lax.broadcasted_iota(jnp.int32, p.shape, 0)
    zero = jnp.zeros((), BF16)
    pv = jnp.zeros(acc_ref.shape, F32)
    for h in range(N_HEADS):
        vh = jnp.concatenate([r[pl.ds(h, PAGE_SIZE, stride=N_HEADS), :] for r in v_refs], axis=0)
        v_hi, v_lo = _split_bf16(vh)
        mine = (prow // 2) == h
        ph_hi = jnp.where(mine, p_hi, zero)
        r = _dot(jnp.concatenate([ph_hi, jnp.where(mine, p_lo, zero)], axis=0), v_hi)
        pv = pv + r[:n_rows] + r[n_rows:] + _dot(ph_hi, v_lo)
    acc_ref[...] = a * acc_ref[...] + pv

    @pl.when(step == pl.num_programs(1) - 1)
    def _():
        o = acc_ref[...] / l_ref[...]
        lam = _diff_lambda(lamv_ref, lam_init)
        outs = []
        for h in range(N_HEADS):
            od = o[2 * h:2 * h + 1] - lam * o[2 * h + 1:2 * h + 2]
            ms = jnp.mean(od * od, axis=-1, keepdims=True)
            outs.append(od * lax.rsqrt(ms + LN_EPS) * sg_ref[...] * (1.0 - lam_init))
        o_ref[...] = jnp.concatenate(outs, axis=1)


def _attn_sample_call(q, k_new, v_new, cache_kt, cache_v2, page_table, layer, p, lam_init):
    n_seq, cols = q.shape
    n_pages = page_table.shape[1]
    npg = PAGES_PER_STEP
    j = layer - N_A_LAYERS
    row3 = lambda a: a.reshape(n_seq, 1, cols)
    row_spec = pl.BlockSpec((None, 1, cols), lambda b, g, pt: (b, 0, 0))

    def page_spec(i, shape):
        return pl.BlockSpec((None,) + shape, lambda b, g, pt: (pt[b * n_pages + g * npg + i], 0, 0))

    grid_spec = pltpu.PrefetchScalarGridSpec(
        num_scalar_prefetch=1,
        grid=(n_seq, n_pages // npg),
        in_specs=[row_spec, row_spec, row_spec, _at((4, HEAD_DIM), j), _at((1, V_DIM), j)]
        + [page_spec(i, cache_kt.shape[1:]) for i in range(npg)]
        + [page_spec(i, cache_v2.shape[1:]) for i in range(npg)],
        out_specs=row_spec,
        scratch_shapes=[pltpu.VMEM((cols, LANES), F32), pltpu.VMEM((2 * N_HEADS, 1), F32),
                        pltpu.VMEM((2 * N_HEADS, 1), F32), pltpu.VMEM((2 * N_HEADS, V_DIM), F32)],
    )
    out = pl.pallas_call(
        functools.partial(_attn_sample_kernel, lam_init=lam_init),
        out_shape=jax.ShapeDtypeStruct((n_seq, 1, cols), F32),
        grid_spec=grid_spec,
        compiler_params=_cparams(2),
        name="attn_sample",
    )(page_table.reshape(-1), row3(q), row3(k_new), row3(v_new), p["lamv"], p["subln_g"],
      *([cache_kt] * npg), *([cache_v2] * npg))
    return out.reshape(n_seq, cols)


def _attn_out_kernel(o_ref, x_ref, gt_ref, wo_ref, ng_ref, nb_ref, y_ref, *, precise):
    y = _mm(o_ref[...], wo_ref, precise)
    x = x_ref[...]
    y_ref[...] = _layer_norm(DEEPNORM_ALPHA * x + (1.0 + gt_ref[...]) * y, ng_ref[...], nb_ref[...])


def _attn_out_call(grp, o, x, layer, p):
    d = x.shape[1]
    cols = o.shape[1]
    j = layer - N_A_LAYERS
    return pl.pallas_call(
        functools.partial(_attn_out_kernel, precise=grp.precise),
        out_shape=jax.ShapeDtypeStruct((grp.n, d), F32),
        grid=(grp.n_tiles,),
        in_specs=[grp.row_spec(cols), grp.row_spec(d), grp.mod_spec(layer, 2), _at((cols, d), j),
                  _at((1, d), layer, 0), _at((1, d), layer, 0)],
        out_specs=grp.row_spec(d),
        compiler_params=_cparams(1),
        name="attn_out",
    )(o, x, grp.mods, p["b_w_o"], p["ln_g"], p["ln_b"])


def _rope_tables(pos):
    half = ROT_DIM // 2
    inv = 1.0 / (ROPE_THETA ** (jnp.arange(0, ROT_DIM, 2, dtype=F32) / ROT_DIM))
    ang = pos.astype(F32)[:, None] * inv[None, :]
    cos, sin = jnp.cos(ang), jnp.sin(ang)
    n = pos.shape[0]
    ones = jnp.ones((n, HEAD_DIM - ROT_DIM), F32)
    zeros = jnp.zeros((n, HEAD_DIM - ROT_DIM), F32)
    zh = jnp.zeros((n, half), F32)
    cos_t = jnp.concatenate([cos, cos, ones], axis=1)
    sin_lo = jnp.concatenate([-sin, zh, zeros], axis=1)
    sin_hi = jnp.concatenate([zh, sin, zeros], axis=1)
    rep = LANES // HEAD_DIM
    return tuple(jnp.tile(t, (1, rep)) for t in (cos_t, sin_lo, sin_hi))


def _trunk(grp, x, p, rope_tabs, attend, expert_tile, seq):
    v_rows = []
    k_sh = v_sh = kv_attn = None
    for l in range(DEPTH):
        if l < N_A_LAYERS:
            x, v_a = _a_mixer_call(grp, x, l, p)
            v_rows.append(v_a)
        else:
            lam_init = 0.8 - 0.6 * math.exp(-0.3 * l)
            q = _q_call(grp, x, l, p, rope_tabs, F32 if seq is None else BF16)
            o = attend(q, kv_attn, l, p, lam_init)
            x = _attn_out_call(grp, o, x, l, p)
        x = _moe_sublayer(grp, x, l, p, expert_tile)
        if l == N_A_LAYERS - 1:
            outs = _kv_call(grp, x, p, rope_tabs, seq)
            k_sh, v_sh = outs[0], outs[1]
            kv_attn = (k_sh, v_sh) if seq is None else (outs[2], outs[3])
    return x, k_sh, v_sh, v_rows


def kernel(x_prompt, x_sample, cache_k, cache_v, page_table, c_prompt, c_sample, ada_w, ada_b, ln_g, ln_b, a_w_in, a_ln_g, a_ln_b, a_w_s, a_b_s, a_w_out, kv_w_k, kv_w_v, b_w_q, b_lam_q1, b_lam_k1, b_lam_q2, b_lam_k2, b_subln_g, b_w_o, router_w, router_bias, moe_w_gate, moe_w_up, moe_w_down):
    batch, seq, d = x_prompt.shape
    dec_batch, dec_seq, _ = x_sample.shape
    assert dec_seq == 1, "the sample group is one new token per sequence"
    n_pool = cache_k.shape[0]
    past = page_table.shape[1] * PAGE_SIZE
    depth, n_exp, _, d_exp = moe_w_gate.shape
    n_a, d_a = a_ln_g.shape
    n_b = b_w_q.shape[0]
    gd = d_a // A_GROUPS
    epg = EXPERTS_PER_GROUP

    p_f32 = dict(
        a_w_in=a_w_in, a_w_out=a_w_out, kv_w_k=kv_w_k, kv_w_v=kv_w_v, b_w_q=b_w_q, b_w_o=b_w_o,
        moe_w_gate=moe_w_gate.reshape(depth, n_exp // epg, epg, d, d_exp),
        moe_w_up=moe_w_up.reshape(depth, n_exp // epg, epg, d, d_exp),
        moe_w_down=moe_w_down.reshape(depth, n_exp // epg, epg, d_exp, d),
        a_ln_g=a_ln_g.reshape(n_a, 1, d_a), a_ln_b=a_ln_b.reshape(n_a, 1, d_a),
        a_w_s=a_w_s, a_b_s_t=jnp.swapaxes(a_b_s, 1, 2),
        a_coef=jnp.repeat(a_w_s[:, :, 0, 0], gd, axis=1).reshape(n_a, 1, d_a),
        a_bias=jnp.repeat(a_b_s[:, :, 0], gd, axis=1).reshape(n_a, 1, d_a),
        ln_g=ln_g.reshape(depth, 2, 1, d), ln_b=ln_b.reshape(depth, 2, 1, d),
        lamv=jnp.stack([b_lam_q1, b_lam_k1, b_lam_q2, b_lam_k2], axis=1),
        subln_g=b_subln_g.reshape(n_b, 1, V_DIM),
        router_wt=router_w.T, router_b=router_bias.reshape(N_EXPERTS, 1),
    )
    matmul_weights = ("a_w_in", "a_w_out", "kv_w_k", "kv_w_v", "b_w_q", "b_w_o",
                      "moe_w_gate", "moe_w_up", "moe_w_down")
    p_bf16 = {k: (v.astype(BF16) if k in matmul_weights else v) for k, v in p_f32.items()}

    mods = _ada_call(jnp.concatenate([c_sample, c_prompt], axis=0), ada_w, ada_b)

    gp = _Group(batch * seq, PROMPT_TILE, seq, dec_batch, mods)
    tabs_p = _rope_tables(jnp.arange(seq, dtype=jnp.int32))

    def attend_prompt(q, kv, layer, p, lam_init):
        return _attn_prompt_call(q, kv[0], kv[1], layer, p, lam_init, batch, seq, ATTN_TILE)

    y_p, kt_p, v_p, _ = _trunk(gp, x_prompt.reshape(batch * seq, d), p_bf16, tabs_p, attend_prompt,
                               PROMPT_EXPERT_TILE, seq)
    k_p = jnp.transpose(kt_p.reshape(batch, N_HEADS, 2, HEAD_DIM, seq), (0, 4, 1, 2, 3))

    gs = _Group(dec_batch, dec_batch, 1, 0, mods)
    tabs_s = _rope_tables(jnp.full((1,), past, dtype=jnp.int32))
    cache_kt = jnp.transpose(cache_k, (0, 2, 3, 4, 1)).reshape(n_pool, N_HEADS * 2 * HEAD_DIM, PAGE_SIZE)
    cache_v2 = cache_v.reshape(n_pool, PAGE_SIZE * N_HEADS, V_DIM)

    def attend_sample(q, kv, layer, p, lam_init):
        return _attn_sample_call(q, kv[0], kv[1], cache_kt, cache_v2, page_table, layer, p, lam_init)

    y_s, k_s, v_s, v_rows = _trunk(gs, x_sample.reshape(dec_batch, d), p_f32, tabs_s, attend_sample,
                                   dec_batch, None)

    return (
        y_p.reshape(batch, seq, d),
        y_s.reshape(dec_batch, 1, d),
        k_p,
        v_p.reshape(batch, seq, N_HEADS, V_DIM),
        k_s.reshape(dec_batch, 1, N_HEADS, 2, HEAD_DIM),
        v_s.reshape(dec_batch, 1, N_HEADS, V_DIM),
        jnp.stack(v_rows, axis=0).reshape(N_A_LAYERS, dec_batch, 1, d_a),
    )
```
